```python
import math
import jax, jax.numpy as jnp
from jax import lax
import numpy as np

D_MODEL = 1024
BATCH = 8
SEQ = 4096
DEPTH = 4

N_MEM = 256
DA_HEADS = 4
DA_HEAD_DIM = 64
DA_V_DIM = 2 * DA_HEAD_DIM
DA_WIDTH = DA_HEADS * DA_V_DIM
HG_HEADS = 4
HG_DK = 128
HG_DV = 128
HG_WIDTH = HG_HEADS * HG_DV
HG_CHUNK = 64
HG_MIN_FORGET = 1e-20
MX_HEADS = 4
MX_HEAD_DIM = 128
MX_WIDTH = MX_HEADS * MX_HEAD_DIM
MIX_WIDTH = DA_WIDTH + HG_WIDTH + MX_WIDTH
IN_WIDTHS = (
    DA_HEADS * 2 * DA_HEAD_DIM,
    DA_HEADS * 2 * DA_HEAD_DIM,
    DA_WIDTH,
    HG_HEADS * HG_DK,
    HG_HEADS * HG_DK,
    HG_HEADS * HG_DK,
    HG_WIDTH,
    HG_WIDTH,
    MX_WIDTH,
)
IN_WIDTH = 4608
ROPE_THETA = 500000.0
ROPE_DIM = DA_HEAD_DIM // 4
Q_BLOCK = 128
D_FF = 2816
CONV_W = 3
LN_EPS = 1e-5
RMS_EPS = 1e-6
DEEPNORM_ALPHA = (2 * DEPTH) ** 0.25
DEEPNORM_BETA = (8 * DEPTH) ** -0.25

kernel_name = "hymba_style_diffattn_hgrn2_memxattn_convffn_encoder"


def layer_norm(x, g, b):
    xf = x.astype(jnp.float32)
    mu = jnp.mean(xf, axis=-1, keepdims=True)
    var = jnp.mean(jnp.square(xf - mu), axis=-1, keepdims=True)
    y = (xf - mu) * lax.rsqrt(var + LN_EPS)
    return (y * g.astype(jnp.float32) + b.astype(jnp.float32)).astype(x.dtype)


def rms_norm(x, g):
    xf = x.astype(jnp.float32)
    y = xf * lax.rsqrt(jnp.mean(jnp.square(xf), axis=-1, keepdims=True) + RMS_EPS)
    return (y * g.astype(jnp.float32)).astype(x.dtype)


def rope_partial(t, cos, sin):
    half = ROPE_DIM // 2
    t1 = t[..., :half]
    t2 = t[..., half:ROPE_DIM]
    return jnp.concatenate([t1 * cos - t2 * sin, t2 * cos + t1 * sin, t[..., ROPE_DIM:]], axis=-1)


def diff_attention(q, k, v, lam):
    B, S, H = q.shape[:3]
    nblk = S // Q_BLOCK
    qb = q.reshape(B, nblk, Q_BLOCK, H, 2, DA_HEAD_DIM).transpose(1, 0, 2, 3, 4, 5)

    def one_block(qi):
        s = jnp.einsum('bqhcd,bkhcd->bhcqk', qi, k).astype(jnp.float32)
        p = jax.nn.softmax(s, axis=-1)
        a = p[:, :, 0] - lam * p[:, :, 1]
        return jnp.einsum('bhqk,bkhv->bqhv', a.astype(v.dtype), v)

    o = lax.map(one_block, qb)
    return o.transpose(1, 0, 2, 3, 4).reshape(B, S, H, DA_V_DIM)


def hgrn2_chunk_scan(q, k, v, g):
    B, S, H, dk = q.shape
    dv = v.shape[-1]
    n = S // HG_CHUNK

    def to_chunks(t):
        return t.reshape(B, n, HG_CHUNK, H, t.shape[-1]).transpose(1, 0, 3, 2, 4)

    qc, kc, vc, gc = to_chunks(q), to_chunks(k), to_chunks(v), to_chunks(g)
    bc = jnp.cumsum(gc, axis=3)
    mask = jnp.tril(jnp.ones((HG_CHUNK, HG_CHUNK), dtype=bool))[:, :, None]

    def step(state, inp):
        qt, kt, vt, bt = inp
        o_inter = jnp.einsum('bhtk,bhkv->bhtv', qt * jnp.exp(bt), state)
        diff = bt[:, :, :, None, :] - bt[:, :, None, :, :]
        dec = jnp.where(mask, jnp.exp(jnp.where(mask, diff, 0.0)), 0.0)
        attn = jnp.einsum('bhtk,bhtsk->bhts', qt, dec * kt[:, :, None, :, :])
        o_intra = jnp.einsum('bhts,bhsv->bhtv', attn, vt)
        b_last = bt[:, :, -1:, :]
        state = jnp.exp(b_last[:, :, 0, :])[..., None] * state + jnp.einsum(
            'bhsk,bhsv->bhkv', kt * jnp.exp(b_last - bt), vt)
        return state, o_inter + o_intra

    s0 = jnp.zeros((B, H, dk, dv), jnp.float32)
    _, o = lax.scan(step, s0, (qc, kc, vc, bc))
    return o.transpose(1, 0, 3, 2, 4).reshape(B, S, H, dv)


def hgrn_lower_bound(lb_logits, layer):
    p = jax.nn.softmax(lb_logits.astype(jnp.float32), axis=0)
    return (jnp.cumsum(p, axis=0) - p[0])[layer]


def hgrn_log_forget(z, lb):
    f = lb + (1.0 - lb) * jax.nn.sigmoid(z)
    return jnp.log(jnp.maximum(f, HG_MIN_FORGET))


def memory_attention(q, mk, mv):
    s = jnp.einsum('bqhd,bmhd->bhqm', q * (MX_HEAD_DIM ** -0.5), mk).astype(jnp.float32)
    p = jax.nn.softmax(s, axis=-1)
    return jnp.einsum('bhqm,bmhd->bqhd', p.astype(mv.dtype), mv)


def depthwise_conv_centred(u, w, b):
    y = lax.conv_general_dilated(
        u, w[:, None, :].astype(u.dtype), window_strides=(1,),
        padding=((CONV_W // 2, CONV_W // 2),),
        dimension_numbers=('NWC', 'WIO', 'NWC'),
        feature_group_count=u.shape[-1])
    return y + b.astype(u.dtype)


def hybrid_layer(x, mem, cos, sin, layer, w_in, da_lambda, da_norm_g, hg_lb_fwd, hg_lb_bwd,
                 hg_norm_g, w_mem_kv, w_out, ln1_g, ln1_b, w_up, conv_w, conv_b, w_down,
                 ln2_g, ln2_b):
    B, S, _ = x.shape
    M = mem.shape[1]
    proj = x @ w_in
    split_idx = [int(i) for i in np.cumsum(IN_WIDTHS)[:-1]]
    da_q, da_k, da_v, hg_q, hg_ff, hg_fb, hg_i, hg_g, mx_q = jnp.split(proj, split_idx, axis=-1)

    dq = rope_partial(da_q.reshape(B, S, DA_HEADS, 2, DA_HEAD_DIM), cos, sin) * (DA_HEAD_DIM ** -0.5)
    dk = rope_partial(da_k.reshape(B, S, DA_HEADS, 2, DA_HEAD_DIM), cos, sin)
    dv = da_v.reshape(B, S, DA_HEADS, DA_V_DIM)
    lam_init = 0.8 - 0.6 * math.exp(-0.3 * layer)
    lf = da_lambda.astype(jnp.float32)
    lam = jnp.exp(jnp.sum(lf[0] * lf[1])) - jnp.exp(jnp.sum(lf[2] * lf[3])) + lam_init
    da_o = diff_attention(dq, dk, dv, lam)
    da_o = (rms_norm(da_o, da_norm_g) * (1.0 - lam_init)).reshape(B, S, DA_WIDTH)

    f32 = jnp.float32
    hq = jax.nn.silu(hg_q.reshape(B, S, HG_HEADS, HG_DK).astype(f32))
    hv = hg_i.reshape(B, S, HG_HEADS, HG_DV).astype(f32)
    lb_f = hgrn_lower_bound(hg_lb_fwd, layer).reshape(HG_HEADS, HG_DK)
    lb_b = hgrn_lower_bound(hg_lb_bwd, layer).reshape(HG_HEADS, HG_DK)
    g_f = hgrn_log_forget(hg_ff.reshape(B, S, HG_HEADS, HG_DK).astype(f32), lb_f)
    g_b = hgrn_log_forget(hg_fb.reshape(B, S, HG_HEADS, HG_DK).astype(f32), lb_b)
    o_fwd = hgrn2_chunk_scan(hq, -jnp.expm1(g_f), hv, g_f)
    o_bwd = jnp.flip(hgrn2_chunk_scan(jnp.flip(hq, 1), jnp.flip(-jnp.expm1(g_b), 1),
                                      jnp.flip(hv, 1), jnp.flip(g_b, 1)), 1)
    hg_o = (o_fwd + o_bwd).astype(x.dtype)
    hg_gate = jax.nn.silu(hg_g.reshape(B, S, HG_HEADS, HG_DV))
    hg_o = (rms_norm(hg_o, hg_norm_g) * hg_gate).reshape(B, S, HG_WIDTH)

    mkv = (mem @ w_mem_kv).reshape(B, M, 2, MX_HEADS, MX_HEAD_DIM)
    mx_o = memory_attention(mx_q.reshape(B, S, MX_HEADS, MX_HEAD_DIM), mkv[:, :, 0], mkv[:, :, 1])
    mx_o = mx_o.reshape(B, S, MX_WIDTH)

    mix = jnp.concatenate([da_o, hg_o, mx_o], axis=-1) @ w_out
    x = layer_norm(DEEPNORM_ALPHA * x + mix, ln1_g, ln1_b)

    u = depthwise_conv_centred(x @ w_up, conv_w, conv_b)
    gate, val = jnp.split(u, 2, axis=-1)
    ffn = (jax.nn.silu(gate) * val) @ w_down
    return layer_norm(DEEPNORM_ALPHA * x + ffn, ln2_g, ln2_b)


def setup_inputs(seed: int = 0) -> dict:
    key = jax.random.key(seed)
    ks = jax.random.split(key, 24)
    nrm = jax.random.normal
    D = D_MODEL
    x = nrm(ks[0], (BATCH, SEQ, D), jnp.float32)
    mem = nrm(ks[1], (BATCH, N_MEM, D), jnp.float32)
    offsets = jax.random.randint(ks[2], (BATCH, 1), 0, 1024, dtype=jnp.int32)
    positions = (jnp.arange(SEQ, dtype=jnp.int32)[None, :] + offsets).astype(jnp.int32)
    return {
        "x": x,
        "mem": mem,
        "positions": positions,
        "ln_in_g": 1.0 + 0.02 * nrm(ks[3], (D,), jnp.float32),
        "ln_in_b": 0.02 * nrm(ks[4], (D,), jnp.float32),
        "w_in": nrm(ks[5], (DEPTH, D, IN_WIDTH), jnp.float32) * D ** -0.5,
        "da_lambda": 0.1 * nrm(ks[6], (DEPTH, 4, DA_HEAD_DIM), jnp.float32),
        "da_norm_g": 1.0 + 0.02 * nrm(ks[7], (DEPTH, DA_V_DIM), jnp.float32),
        "hg_lb_fwd": 0.5 * nrm(ks[8], (DEPTH, HG_HEADS * HG_DK), jnp.float32),
        "hg_lb_bwd": 0.5 * nrm(ks[9], (DEPTH, HG_HEADS * HG_DK), jnp.float32),
        "hg_norm_g": 1.0 + 0.02 * nrm(ks[10], (DEPTH, HG_DV), jnp.float32),
        "w_mem_kv": nrm(ks[11], (DEPTH, D, 2 * MX_WIDTH), jnp.float32) * D ** -0.5,
        "w_out": nrm(ks[12], (DEPTH, MIX_WIDTH, D), jnp.float32) * (MIX_WIDTH ** -0.5 * DEEPNORM_BETA),
        "ln1_g": 1.0 + 0.02 * nrm(ks[13], (DEPTH, D), jnp.float32),
        "ln1_b": 0.02 * nrm(ks[14], (DEPTH, D), jnp.float32),
        "w_up": nrm(ks[15], (DEPTH, D, 2 * D_FF), jnp.float32) * D ** -0.5,
        "conv_w": nrm(ks[16], (DEPTH, CONV_W, 2 * D_FF), jnp.float32) * CONV_W ** -0.5,
        "conv_b": 0.02 * nrm(ks[17], (DEPTH, 2 * D_FF), jnp.float32),
        "w_down": nrm(ks[18], (DEPTH, D_FF, D), jnp.float32) * (D_FF ** -0.5 * DEEPNORM_BETA),
        "ln2_g": 1.0 + 0.02 * nrm(ks[19], (DEPTH, D), jnp.float32),
        "ln2_b": 0.02 * nrm(ks[20], (DEPTH, D), jnp.float32),
    }


def reference(x, mem, positions, ln_in_g, ln_in_b, w_in, da_lambda, da_norm_g, hg_lb_fwd,
              hg_lb_bwd, hg_norm_g, w_mem_kv, w_out, ln1_g, ln1_b, w_up, conv_w, conv_b,
              w_down, ln2_g, ln2_b):
    inv_freq = ROPE_THETA ** (-jnp.arange(0, ROPE_DIM, 2, dtype=jnp.float32) / ROPE_DIM)
    ang = positions.astype(jnp.float32)[..., None] * inv_freq
    cos = jnp.cos(ang)[:, :, None, None, :].astype(x.dtype)
    sin = jnp.sin(ang)[:, :, None, None, :].astype(x.dtype)

    h = layer_norm(x, ln_in_g, ln_in_b)
    for l in range(DEPTH):
        h = hybrid_layer(h, mem, cos, sin, l, w_in[l], da_lambda[l], da_norm_g[l],
                         hg_lb_fwd, hg_lb_bwd, hg_norm_g[l], w_mem_kv[l], w_out[l],
                         ln1_g[l], ln1_b[l], w_up[l], conv_w[l], conv_b[l], w_down[l],
                         ln2_g[l], ln2_b[l])
    return h
```

```python
import functools
import math

import jax
import jax.numpy as jnp
from jax import lax
from jax.experimental import pallas as pl
from jax.experimental.pallas import tpu as pltpu

F32 = jnp.float32
BF16 = jnp.bfloat16

D_MODEL = 1024
DEPTH = 4
DA_HEADS = 4
DA_HEAD_DIM = 64
HG_HEADS = 4
HG_MIN_FORGET = 1e-20
MX_HEADS = 4
MX_HEAD_DIM = 128
ROPE_THETA = 500000.0
ROPE_DIM = DA_HEAD_DIM // 4
D_FF = 2816
LN_EPS = 1e-5
RMS_EPS = 1e-6
DEEPNORM_ALPHA = (2 * DEPTH) ** 0.25

LANES = 128
SEC = 512
N_SEC = 9
HG_CHUNK = 128
VMEM_LIMIT = 56 * 1024 * 1024

NT_DIMS = (((1,), (1,)), ((), ()))
TN_DIMS = (((0,), (0,)), ((), ()))


def _cparams(sem):
    return pltpu.CompilerParams(dimension_semantics=sem, vmem_limit_bytes=VMEM_LIMIT)


def _layer_norm_rows(y, g, b):
    mu = jnp.mean(y, axis=-1, keepdims=True)
    d = y - mu
    var = jnp.mean(d * d, axis=-1, keepdims=True)
    return d * lax.rsqrt(var + LN_EPS) * g + b


def _sigmoid(z):
    return 1.0 / (1.0 + jnp.exp(-z))


def _ln_kernel(x_ref, g_ref, b_ref, o_ref, ob_ref):
    y = _layer_norm_rows(x_ref[...], g_ref[...], b_ref[...])
    o_ref[...] = y
    ob_ref[...] = y.astype(BF16)


def _ln_in(x2, g, b, tm):
    T, D = x2.shape
    return pl.pallas_call(
        _ln_kernel,
        grid=(T // tm,),
        in_specs=[pl.BlockSpec((tm, D), lambda i: (i, 0)),
                  pl.BlockSpec((1, D), lambda i: (0, 0)),
                  pl.BlockSpec((1, D), lambda i: (0, 0))],
        out_specs=[pl.BlockSpec((tm, D), lambda i: (i, 0)),
                   pl.BlockSpec((tm, D), lambda i: (i, 0))],
        out_shape=[jax.ShapeDtypeStruct((T, D), F32), jax.ShapeDtypeStruct((T, D), BF16)],
        compiler_params=_cparams(("parallel",)),
        name="ln_in",
    )(x2, g.reshape(1, D), b.reshape(1, D))


def _lower_bound(lb_ref, layer):
    z = lb_ref[...]
    e = jnp.exp(z - jnp.max(z, axis=0, keepdims=True))
    p = e / jnp.sum(e, axis=0, keepdims=True)
    lb = jnp.zeros((1, z.shape[1]), F32)
    for j in range(1, layer + 1):
        lb = lb + p[j:j + 1, :]
    return lb


def _inproj_kernel(x_ref, w_ref, cos_ref, sin_ref, lbf_ref, lbb_ref, o_ref, *, layer):
    n = pl.program_id(1)
    acc = jnp.dot(x_ref[...], w_ref[...], preferred_element_type=F32)

    def rope(scale):
        cs = cos_ref[...]
        sn = sin_ref[...]
        lane = lax.broadcasted_iota(jnp.int32, cs.shape, 1)
        first = (lane & 15) < 8
        for j in range(SEC // LANES):
            t = acc[:, j * LANES:(j + 1) * LANES]
            partner = jnp.where(first, pltpu.roll(t, LANES - 8, axis=1), pltpu.roll(t, 8, axis=1))
            o_ref[:, j * LANES:(j + 1) * LANES] = (t * cs + partner * sn) * scale

    def log_forget(lb_ref):
        lb = _lower_bound(lb_ref, layer)
        f = lb + (1.0 - lb) * _sigmoid(acc)
        o_ref[...] = jnp.log(jnp.maximum(f, HG_MIN_FORGET))

    @pl.when(n == 0)
    def _():
        rope(DA_HEAD_DIM ** -0.5)

    @pl.when(n == 1)
    def _():
        rope(1.0)

    @pl.when((n == 2) | (n == 6))
    def _():
        o_ref[...] = acc

    @pl.when((n == 3) | (n == 7))
    def _():
        o_ref[...] = acc * _sigmoid(acc)

    @pl.when(n == 4)
    def _():
        log_forget(lbf_ref)

    @pl.when(n == 5)
    def _():
        log_forget(lbb_ref)

    @pl.when(n == 8)
    def _():
        o_ref[...] = acc * (MX_HEAD_DIM ** -0.5)


def _inproj(xb, w, cosf, sinf, lbf, lbb, layer, tm):
    T, D = xb.shape
    return pl.pallas_call(
        functools.partial(_inproj_kernel, layer=layer),
        grid=(T // tm, N_SEC),
        in_specs=[pl.BlockSpec((tm, D), lambda i, n: (i, 0)),
                  pl.BlockSpec((D, SEC), lambda i, n: (0, n)),
                  pl.BlockSpec((tm, LANES), lambda i, n: (i, 0)),
                  pl.BlockSpec((tm, LANES), lambda i, n: (i, 0)),
                  pl.BlockSpec((DEPTH, SEC), lambda i, n: (0, 0)),
                  pl.BlockSpec((DEPTH, SEC), lambda i, n: (0, 0))],
        out_specs=pl.BlockSpec((tm, SEC), lambda i, n: (i, n)),
        out_shape=jax.ShapeDtypeStruct((T, N_SEC * SEC), F32),
        compiler_params=_cparams(("parallel", "arbitrary")),
        name="inproj",
    )(xb, w, cosf, sinf, lbf, lbb)


def _attn_kernel(lam_ref, g_ref, q_ref, k_ref, v_ref, o_ref, kb_ref, vb_ref, *, lam_init):
    @pl.when(pl.program_id(2) == 0)
    def _():
        kb_ref[...] = k_ref[...].astype(BF16)
        vb_ref[...] = v_ref[...].astype(BF16)

    lf = lam_ref[...]
    lam = (jnp.exp(jnp.sum(lf[0:1, :] * lf[1:2, :], axis=-1, keepdims=True))
           - jnp.exp(jnp.sum(lf[2:3, :] * lf[3:4, :], axis=-1, keepdims=True)) + lam_init)

    q = q_ref[...]
    lane = lax.broadcasted_iota(jnp.int32, q.shape, 1)
    kb = kb_ref[...]

    def probs(qc):
        s = lax.dot_general(qc.astype(BF16), kb, NT_DIMS, preferred_element_type=F32)
        p = jnp.exp(s - jnp.max(s, axis=-1, keepdims=True))
        return p, jnp.sum(p, axis=-1, keepdims=True)

    p0, l0 = probs(jnp.where(lane < DA_HEAD_DIM, q, 0.0))
    p1, l1 = probs(jnp.where(lane >= DA_HEAD_DIM, q, 0.0))
    a = p0 * (1.0 / l0) - p1 * (lam / l1)
    o = jnp.dot(a.astype(BF16), vb_ref[...], preferred_element_type=F32)
    y = o * lax.rsqrt(jnp.mean(o * o, axis=-1, keepdims=True) + RMS_EPS)
    o_ref[...] = y * g_ref[...] * (1.0 - lam_init)


def _diff_attn(proj3, da_lambda, da_norm_g, layer, tq):
    B, S, _ = proj3.shape
    lam_init = 0.8 - 0.6 * math.exp(-0.3 * layer)
    return pl.pallas_call(
        functools.partial(_attn_kernel, lam_init=lam_init),
        grid=(B, DA_HEADS, S // tq),
        in_specs=[pl.BlockSpec((4, DA_HEAD_DIM), lambda b, h, i: (0, 0)),
                  pl.BlockSpec((1, LANES), lambda b, h, i: (0, 0)),
                  pl.BlockSpec((None, tq, LANES), lambda b, h, i: (b, i, h)),
                  pl.BlockSpec((None, S, LANES), lambda b, h, i: (b, 0, 4 + h)),
                  pl.BlockSpec((None, S, LANES), lambda b, h, i: (b, 0, 8 + h))],
        out_specs=pl.BlockSpec((None, tq, LANES), lambda b, h, i: (b, i, h)),
        out_shape=jax.ShapeDtypeStruct((B, S, DA_HEADS * LANES), F32),
        scratch_shapes=[pltpu.VMEM((S, LANES), BF16), pltpu.VMEM((S, LANES), BF16)],
        compiler_params=_cparams(("parallel", "parallel", "arbitrary")),
        name="diff_attn",
    )(da_lambda, da_norm_g.reshape(1, LANES), proj3, proj3, proj3)


def _bcast_rows(b, idxs, rep):
    return jnp.concatenate(
        [jnp.broadcast_to(b[r:r + 1, :], (rep, b.shape[1])) for r in idxs], axis=0)


def _ref_rows(b, m, reverse, row):
    C = b.shape[0]
    off = m if reverse else m - 1
    if m >= 4:
        return _bcast_rows(b, [G * 2 * m + off for G in range(C // (2 * m))], 2 * m)
    lo = _bcast_rows(b, [8 * G + off for G in range(C // 8)], 8)
    hi = _bcast_rows(b, [8 * G + 4 + off for G in range(C // 8)], 8)
    return jnp.where((row & 7) < 4, lo, hi)


def _hgrn_chunk(q, g, v, state_t, tri, reverse):
    C = q.shape[0]
    row = lax.broadcasted_iota(jnp.int32, (C, C), 0)
    col = lax.broadcasted_iota(jnp.int32, (C, C), 1)
    f = jnp.exp(g)
    k = 1.0 - f

    g1 = g.astype(BF16)
    r1 = g - g1.astype(F32)
    g2 = r1.astype(BF16)
    g3 = (r1 - g2.astype(F32)).astype(BF16)
    cs = jnp.dot(tri, jnp.concatenate([g1, g2, g3], axis=1), preferred_element_type=F32)
    b = cs[:, :LANES] + cs[:, LANES:2 * LANES] + cs[:, 2 * LANES:]

    edge = 0 if reverse else C - 1
    b_edge = jnp.broadcast_to(b[edge:edge + 1, :], b.shape)
    q_hat = q * jnp.exp(b)
    k_hat = k * jnp.exp(b_edge - b)

    attn = jnp.where(row == col,
                     lax.dot_general(q.astype(BF16), k.astype(BF16), NT_DIMS,
                                     preferred_element_type=F32), 0.0)
    m = 1
    while m < C:
        shift = m.bit_length() - 1
        is_query = ((row >> shift) & 1) == (0 if reverse else 1)
        if m == 1:
            qt = jnp.where(is_query, q * f, 0.0)
            kt = jnp.where(is_query, 0.0, k)
        else:
            ref = _ref_rows(b, m, reverse, row)
            e = jnp.exp(jnp.where(is_query, b - ref, ref - b))
            qt = jnp.where(is_query, q * e, 0.0)
            kt = jnp.where(is_query, 0.0, k * e)
        s = lax.dot_general(qt.astype(BF16), kt.astype(BF16), NT_DIMS, preferred_element_type=F32)
        attn = attn + jnp.where((row >> (shift + 1)) == (col >> (shift + 1)), s, 0.0)
        m *= 2

    o = (jnp.dot(attn.astype(BF16), v.astype(BF16), preferred_element_type=F32)
         + lax.dot_general(q_hat.astype(BF16), state_t.astype(BF16), NT_DIMS,
                           preferred_element_type=F32))
    upd = lax.dot_general(v.astype(BF16), k_hat.astype(BF16), TN_DIMS, preferred_element_type=F32)
    new_state = state_t * jnp.exp(b[edge:edge + 1, :]) + upd
    return o, new_state


def _hgrn_kernel(q_ref, gf_ref, gb_ref, v_ref, of_ref, ob_ref):
    S = q_ref.shape[0]
    C = HG_CHUNK
    n = S // C
    row = lax.broadcasted_iota(jnp.int32, (C, C), 0)
    col = lax.broadcasted_iota(jnp.int32, (C, C), 1)
    tri_f = jnp.where(col <= row, 1.0, 0.0).astype(BF16)
    tri_b = jnp.where(col >= row, 1.0, 0.0).astype(BF16)

    def body(c, carry):
        st_f, st_b = carry
        sf = pl.multiple_of(c * C, C)
        sb = pl.multiple_of((n - 1 - c) * C, C)
        o_f, st_f = _hgrn_chunk(q_ref[pl.ds(sf, C), :], gf_ref[pl.ds(sf, C), :],
                                v_ref[pl.ds(sf, C), :], st_f, tri_f, False)
        of_ref[pl.ds(sf, C), :] = o_f
        o_b, st_b = _hgrn_chunk(q_ref[pl.ds(sb, C), :], gb_ref[pl.ds(sb, C), :],
                                v_ref[pl.ds(sb, C), :], st_b, tri_b, True)
        ob_ref[pl.ds(sb, C), :] = o_b
        return st_f, st_b

    z = jnp.zeros((LANES, LANES), F32)
    lax.fori_loop(0, n, body, (z, z))


def _hgrn(proj3):
    B, S, _ = proj3.shape
    spec = lambda blk: pl.BlockSpec((None, S, LANES), lambda b, h: (b, 0, blk + h))
    out = jax.ShapeDtypeStruct((B, S, HG_HEADS * LANES), F32)
    return pl.pallas_call(
        _hgrn_kernel,
        grid=(B, HG_HEADS),
        in_specs=[spec(12), spec(16), spec(20), spec(24)],
        out_specs=[pl.BlockSpec((None, S, LANES), lambda b, h: (b, 0, h)),
                   pl.BlockSpec((None, S, LANES), lambda b, h: (b, 0, h))],
        out_shape=[out, out],
        compiler_params=_cparams(("parallel", "parallel")),
        name="hgrn2",
    )(proj3, proj3, proj3, proj3)


def _mm_kernel(x_ref, w_ref, o_ref):
    o_ref[...] = jnp.dot(x_ref[...].astype(BF16), w_ref[...], preferred_element_type=F32)


def _matmul(x, w, tm, tn):
    M, K = x.shape
    N = w.shape[1]
    return pl.pallas_call(
        _mm_kernel,
        grid=(M // tm, N // tn),
        in_specs=[pl.BlockSpec((tm, K), lambda i, j: (i, 0)),
                  pl.BlockSpec((K, tn), lambda i, j: (0, j))],
        out_specs=pl.BlockSpec((tm, tn), lambda i, j: (i, j)),
        out_shape=jax.ShapeDtypeStruct((M, N), F32),
        compiler_params=_cparams(("parallel", "parallel")),
        name="mem_kv_proj",
    )(x, w)


def _memattn_kernel(q_ref, mk_ref, mv_ref, o_ref):
    for h in range(MX_HEADS):
        sl = slice(h * LANES, (h + 1) * LANES)
        s = lax.dot_general(q_ref[:, sl].astype(BF16), mk_ref[:, sl].astype(BF16), NT_DIMS,
                            preferred_element_type=F32)
        p = jnp.exp(s - jnp.max(s, axis=-1, keepdims=True))
        p = p * (1.0 / jnp.sum(p, axis=-1, keepdims=True))
        o_ref[:, sl] = jnp.dot(p.astype(BF16), mv_ref[:, sl].astype(BF16),
                               preferred_element_type=F32)


def _mem_attn(proj3, mkv3, tq):
    B, S, _ = proj3.shape
    M = mkv3.shape[1]
    W = MX_HEADS * LANES
    return pl.pallas_call(
        _memattn_kernel,
        grid=(B, S // tq),
        in_specs=[pl.BlockSpec((None, tq, W), lambda b, i: (b, i, 8)),
                  pl.BlockSpec((None, M, W), lambda b, i: (b, 0, 0)),
                  pl.BlockSpec((None, M, W), lambda b, i: (b, 0, 1))],
        out_specs=pl.BlockSpec((None, tq, W), lambda b, i: (b, i, 0)),
        out_shape=jax.ShapeDtypeStruct((B, S, W), F32),
        compiler_params=_cparams(("parallel", "parallel")),
        name="mem_attn",
    )(proj3, mkv3, mkv3)


def _outproj_kernel(da_ref, of_ref, ob_ref, gate_ref, hgn_ref, mx_ref, w_ref, x_ref, g_ref, b_ref,
                    o_ref, obf_ref):
    hg = of_ref[...] + ob_ref[...]
    gate = gate_ref[...]
    parts = [da_ref[...].astype(BF16)]
    for h in range(HG_HEADS):
        sl = slice(h * LANES, (h + 1) * LANES)
        y = hg[:, sl]
        y = y * lax.rsqrt(jnp.mean(y * y, axis=-1, keepdims=True) + RMS_EPS) * hgn_ref[...]
        parts.append((y * gate[:, sl]).astype(BF16))
    parts.append(mx_ref[...].astype(BF16))
    mix = jnp.dot(jnp.concatenate(parts, axis=1), w_ref[...], preferred_element_type=F32)
    y = _layer_norm_rows(DEEPNORM_ALPHA * x_ref[...] + mix, g_ref[...], b_ref[...])
    o_ref[...] = y
    obf_ref[...] = y.astype(BF16)


def _outproj(da_o, o_f, o_b, proj, hg_norm_g, mx_o, w_out, x, g, b, tm):
    T, D = x.shape
    W = SEC
    row = lambda i: (i, 0)
    const = lambda i: (0, 0)
    return pl.pallas_call(
        _outproj_kernel,
        grid=(T // tm,),
        in_specs=[pl.BlockSpec((tm, W), row), pl.BlockSpec((tm, W), row), pl.BlockSpec((tm, W), row),
                  pl.BlockSpec((tm, W), lambda i: (i, 7)),
                  pl.BlockSpec((1, LANES), const),
                  pl.BlockSpec((tm, W), row),
                  pl.BlockSpec((3 * W, D), const),
                  pl.BlockSpec((tm, D), row),
                  pl.BlockSpec((1, D), const), pl.BlockSpec((1, D), const)],
        out_specs=[pl.BlockSpec((tm, D), row), pl.BlockSpec((tm, D), row)],
        out_shape=[jax.ShapeDtypeStruct((T, D), F32), jax.ShapeDtypeStruct((T, D), BF16)],
        compiler_params=_cparams(("parallel",)),
        name="outproj_ln",
    )(da_o, o_f, o_b, proj, hg_norm_g.reshape(1, LANES), mx_o, w_out, x, g.reshape(1, D),
      b.reshape(1, D))


FF_TILE = 256
HALO = 16


def _ffn_kernel(xp_ref, xm_ref, xn_ref, xr_ref, wg_ref, wv_ref, cwg_ref, cwv_ref, cbg_ref, cbv_ref,
                wd_ref, g_ref, b_ref, o_ref, obf_ref, xcat_ref, acc_ref, ug_ref, uv_ref,
                *, tiles_per_seq):
    i = pl.program_id(0)
    j = pl.program_id(1)
    tm = xm_ref.shape[0]

    @pl.when(j == 0)
    def _():
        first = (i % tiles_per_seq) == 0
        last = (i % tiles_per_seq) == tiles_per_seq - 1
        xcat_ref[0:HALO, :] = jnp.where(first, jnp.zeros_like(xp_ref[...]), xp_ref[...])
        xcat_ref[HALO:HALO + tm, :] = xm_ref[...]
        xcat_ref[HALO + tm:, :] = jnp.where(last, jnp.zeros_like(xn_ref[...]), xn_ref[...])

    xc = xcat_ref[...]
    ug_ref[...] = jnp.dot(xc, wg_ref[...], preferred_element_type=F32)
    uv_ref[...] = jnp.dot(xc, wv_ref[...], preferred_element_type=F32)

    def conv(u_ref, cw_ref, cb_ref):
        cw = cw_ref[...]
        return (u_ref[pl.ds(HALO - 1, tm), :] * cw[0:1, :] + u_ref[pl.ds(HALO, tm), :] * cw[1:2, :]
                + u_ref[pl.ds(HALO + 1, tm), :] * cw[2:3, :] + cb_ref[...])

    gate = conv(ug_ref, cwg_ref, cbg_ref)
    val = conv(uv_ref, cwv_ref, cbv_ref)
    act = (gate * _sigmoid(gate) * val).astype(BF16)
    part = jnp.dot(act, wd_ref[...], preferred_element_type=F32)

    @pl.when(j == 0)
    def _():
        acc_ref[...] = part

    @pl.when(j > 0)
    def _():
        acc_ref[...] += part

    @pl.when(j == pl.num_programs(1) - 1)
    def _():
        y = _layer_norm_rows(DEEPNORM_ALPHA * xr_ref[...] + acc_ref[...], g_ref[...], b_ref[...])
        o_ref[...] = y
        obf_ref[...] = y.astype(BF16)


def _ffn(xb, x, w_up, conv_w, conv_b, w_down, g, b, tm, seq):
    T, D = x.shape
    nff = D_FF // FF_TILE
    hb = tm // HALO
    nhb = T // HALO
    row = lambda i, j: (i, 0)
    const = lambda i, j: (0, 0)
    return pl.pallas_call(
        functools.partial(_ffn_kernel, tiles_per_seq=seq // tm),
        grid=(T // tm, nff),
        in_specs=[pl.BlockSpec((HALO, D), lambda i, j: (jnp.maximum(i * hb - 1, 0), 0)),
                  pl.BlockSpec((tm, D), row),
                  pl.BlockSpec((HALO, D), lambda i, j: (jnp.minimum((i + 1) * hb, nhb - 1), 0)),
                  pl.BlockSpec((tm, D), row),
                  pl.BlockSpec((D, FF_TILE), lambda i, j: (0, j)),
                  pl.BlockSpec((D, FF_TILE), lambda i, j: (0, nff + j)),
                  pl.BlockSpec((3, FF_TILE), lambda i, j: (0, j)),
                  pl.BlockSpec((3, FF_TILE), lambda i, j: (0, nff + j)),
                  pl.BlockSpec((1, FF_TILE), lambda i, j: (0, j)),
                  pl.BlockSpec((1, FF_TILE), lambda i, j: (0, nff + j)),
                  pl.BlockSpec((FF_TILE, D), lambda i, j: (j, 0)),
                  pl.BlockSpec((1, D), const), pl.BlockSpec((1, D), const)],
        out_specs=[pl.BlockSpec((tm, D), row), pl.BlockSpec((tm, D), row)],
        out_shape=[jax.ShapeDtypeStruct((T, D), F32), jax.ShapeDtypeStruct((T, D), BF16)],
        scratch_shapes=[pltpu.VMEM((tm + 2 * HALO, D), BF16),
                        pltpu.VMEM((tm, D), F32),
                        pltpu.VMEM((tm + 2 * HALO, FF_TILE), F32),
                        pltpu.VMEM((tm + 2 * HALO, FF_TILE), F32)],
        compiler_params=_cparams(("parallel", "arbitrary")),
        name="conv_ffn_ln",
    )(xb, xb, xb, x, w_up, w_up, conv_w, conv_w, conv_b.reshape(1, -1), conv_b.reshape(1, -1),
      w_down, g.reshape(1, D), b.reshape(1, D))


def _rope_tables(positions):
    half = ROPE_DIM // 2
    inv_freq = ROPE_THETA ** (-jnp.arange(0, ROPE_DIM, 2, dtype=F32) / ROPE_DIM)
    ang = positions.astype(F32)[..., None] * inv_freq
    cos, sin = jnp.cos(ang), jnp.sin(ang)
    pad = DA_HEAD_DIM - ROPE_DIM
    shape = cos.shape[:-1] + (pad,)
    cos64 = jnp.concatenate([cos, cos, jnp.ones(shape, F32)], axis=-1)
    sin64 = jnp.concatenate([-sin, sin, jnp.zeros(shape, F32)], axis=-1)
    cosf = jnp.concatenate([cos64, cos64], axis=-1)
    sinf = jnp.concatenate([sin64, sin64], axis=-1)
    T = cosf.shape[0] * cosf.shape[1]
    return cosf.reshape(T, LANES), sinf.reshape(T, LANES)


def _pick(n, target):
    t = min(n, target)
    while n % t:
        t //= 2
    return t


def kernel(x, mem, positions, ln_in_g, ln_in_b, w_in, da_lambda, da_norm_g, hg_lb_fwd, hg_lb_bwd,
           hg_norm_g, w_mem_kv, w_out, ln1_g, ln1_b, w_up, conv_w, conv_b, w_down, ln2_g, ln2_b):
    B, S, D = x.shape
    M = mem.shape[1]
    T = B * S
    tm = _pick(S, 1024)
    cosf, sinf = _rope_tables(positions)
    w_in_b, w_kv_b, w_out_b = w_in.astype(BF16), w_mem_kv.astype(BF16), w_out.astype(BF16)
    w_up_b, w_down_b = w_up.astype(BF16), w_down.astype(BF16)
    mem2 = mem.reshape(B * M, D)

    h, hb = _ln_in(x.reshape(T, D), ln_in_g, ln_in_b, tm)
    for l in range(DEPTH):
        proj = _inproj(hb, w_in_b[l], cosf, sinf, hg_lb_fwd, hg_lb_bwd, l, tm)
        proj3 = proj.reshape(B, S, N_SEC * SEC)
        da_o = _diff_attn(proj3, da_lambda[l], da_norm_g[l], l, _pick(S, 256))
        o_f, o_b = _hgrn(proj3)
        mkv = _matmul(mem2, w_kv_b[l], _pick(B * M, 512), 512)
        mx_o = _mem_attn(proj3, mkv.reshape(B, M, 2 * SEC), _pick(S, 512))
        h, hb = _outproj(da_o.reshape(T, SEC), o_f.reshape(T, SEC), o_b.reshape(T, SEC), proj,
                         hg_norm_g[l], mx_o.reshape(T, SEC), w_out_b[l], h, ln1_g[l], ln1_b[l],
                         _pick(S, 512))
        h, hb = _ffn(hb, h, w_up_b[l], conv_w[l], conv_b[l], w_down_b[l], ln2_g[l], ln2_b[l],
                     tm, S)
    return h.reshape(B, S, D)
```

```python
import functools
import math

import jax
import jax.numpy as jnp
from jax import lax
from jax.experimental import pallas as pl
from jax.experimental.pallas import tpu as pltpu

F32 = jnp.float32
BF16 = jnp.bfloat16

D_MODEL = 1024
DEPTH = 4
DA_HEADS = 4
DA_HEAD_DIM = 64
HG_HEADS = 4
HG_MIN_FORGET = 1e-20
MX_HEADS = 4
MX_HEAD_DIM = 128
ROPE_THETA = 500000.0
ROPE_DIM = DA_HEAD_DIM // 4
D_FF = 2816
LN_EPS = 1e-5
RMS_EPS = 1e-6
DEEPNORM_ALPHA = (2 * DEPTH) ** 0.25
LOG2E = 1.4426950408889634

LANES = 128
SEC = 512
N_SEC = 9
HG_CHUNK = 128
VMEM_LIMIT = 56 * 1024 * 1024

NT_DIMS = (((1,), (1,)), ((), ()))
TN_DIMS = (((0,), (0,)), ((), ()))


def _cparams(sem):
    return pltpu.CompilerParams(dimension_semantics=sem, vmem_limit_bytes=VMEM_LIMIT)


def _layer_norm_rows(y, g, b):
    mu = jnp.mean(y, axis=-1, keepdims=True)
    d = y - mu
    var = jnp.mean(d * d, axis=-1, keepdims=True)
    return d * lax.rsqrt(var + LN_EPS) * g + b


def _sigmoid(z):
    return 1.0 / (1.0 + jnp.exp(-z))


def _ln_kernel(x_ref, g_ref, b_ref, o_ref, ob_ref):
    y = _layer_norm_rows(x_ref[...], g_ref[...], b_ref[...])
    o_ref[...] = y
    ob_ref[...] = y.astype(BF16)


def _ln_in(x2, g, b, tm):
    T, D = x2.shape
    return pl.pallas_call(
        _ln_kernel,
        grid=(T // tm,),
        in_specs=[pl.BlockSpec((tm, D), lambda i: (i, 0)),
                  pl.BlockSpec((1, D), lambda i: (0, 0)),
                  pl.BlockSpec((1, D), lambda i: (0, 0))],
        out_specs=[pl.BlockSpec((tm, D), lambda i: (i, 0)),
                   pl.BlockSpec((tm, D), lambda i: (i, 0))],
        out_shape=[jax.ShapeDtypeStruct((T, D), F32), jax.ShapeDtypeStruct((T, D), BF16)],
        compiler_params=_cparams(("parallel",)),
        name="ln_in",
    )(x2, g.reshape(1, D), b.reshape(1, D))


def _lower_bound(lb_ref, layer):
    z = lb_ref[...]
    e = jnp.exp(z - jnp.max(z, axis=0, keepdims=True))
    p = e / jnp.sum(e, axis=0, keepdims=True)
    lb = jnp.zeros((1, z.shape[1]), F32)
    for j in range(1, layer + 1):
        lb = lb + p[j:j + 1, :]
    return lb


def _inproj_kernel(x_ref, w_ref, cos_ref, sin_ref, lbf_ref, lbb_ref, a_ref, g_ref, c_ref,
                   *, layer, sub):
    n = pl.program_id(1)
    tm = x_ref.shape[0]

    def run(epilogue, out_ref):
        for r in range(tm // sub):
            rows = slice(r * sub, (r + 1) * sub)
            acc = jnp.dot(x_ref[rows, :], w_ref[...], preferred_element_type=F32)
            out_ref[rows, :] = epilogue(acc, rows).astype(out_ref.dtype)

    def rope(scale):
        def epilogue(acc, rows):
            cs = cos_ref[rows, :]
            sn = sin_ref[rows, :]
            lane = lax.broadcasted_iota(jnp.int32, cs.shape, 1)
            first = (lane & 15) < 8
            parts = []
            for j in range(SEC // LANES):
                t = acc[:, j * LANES:(j + 1) * LANES]
                partner = jnp.where(first, pltpu.roll(t, LANES - 8, axis=1), pltpu.roll(t, 8, axis=1))
                parts.append((t * cs + partner * sn) * scale)
            return jnp.concatenate(parts, axis=1)
        return epilogue

    def log_forget(lb_ref):
        def epilogue(acc, rows):
            lb = _lower_bound(lb_ref, layer)
            f = lb + (1.0 - lb) * _sigmoid(acc)
            return jnp.log(jnp.maximum(f, HG_MIN_FORGET))
        return epilogue

    identity = lambda acc, rows: acc
    silu = lambda acc, rows: acc * _sigmoid(acc)
    branches = [
        (rope(DA_HEAD_DIM ** -0.5 * LOG2E), a_ref),
        (rope(1.0), a_ref),
        (identity, a_ref),
        (silu, a_ref),
        (log_forget(lbf_ref), g_ref),
        (log_forget(lbb_ref), g_ref),
        (identity, c_ref),
        (silu, c_ref),
        (lambda acc, rows: acc * (MX_HEAD_DIM ** -0.5), c_ref),
    ]
    for k, (epilogue, out_ref) in enumerate(branches):
        pl.when(n == k)(functools.partial(run, epilogue, out_ref))


def _inproj(xb, w, cosf, sinf, lbf, lbb, layer, tm):
    T, D = xb.shape
    return pl.pallas_call(
        functools.partial(_inproj_kernel, layer=layer, sub=min(tm, 256)),
        grid=(T // tm, N_SEC),
        in_specs=[pl.BlockSpec((tm, D), lambda i, n: (i, 0)),
                  pl.BlockSpec((D, SEC), lambda i, n: (0, n)),
                  pl.BlockSpec((tm, LANES), lambda i, n: (i, 0)),
                  pl.BlockSpec((tm, LANES), lambda i, n: (i, 0)),
                  pl.BlockSpec((DEPTH, SEC), lambda i, n: (0, 0)),
                  pl.BlockSpec((DEPTH, SEC), lambda i, n: (0, 0))],
        out_specs=[pl.BlockSpec((tm, SEC), lambda i, n: (i, jnp.minimum(n, 3))),
                   pl.BlockSpec((tm, SEC), lambda i, n: (i, jnp.clip(n - 4, 0, 1))),
                   pl.BlockSpec((tm, SEC), lambda i, n: (i, jnp.clip(n - 6, 0, 2)))],
        out_shape=[jax.ShapeDtypeStruct((T, 4 * SEC), BF16),
                   jax.ShapeDtypeStruct((T, 2 * SEC), F32),
                   jax.ShapeDtypeStruct((T, 3 * SEC), BF16)],
        compiler_params=_cparams(("parallel", "arbitrary")),
        name="inproj",
    )(xb, w, cosf, sinf, lbf, lbb)


def _attn_kernel(lam_ref, g_ref, q_ref, k_ref, v_ref, o_ref, vt_ref, sa_ref, sb_ref,
                 *, lam_init, tq, kb):
    S = k_ref.shape[0]
    nq = S // tq
    for j in range(S // kb):
        vt_ref[:, j * kb:(j + 1) * kb] = v_ref[j * kb:(j + 1) * kb, :].astype(F32).T.astype(BF16)

    lf = lam_ref[...]
    lam = (jnp.exp(jnp.sum(lf[0:1, :] * lf[1:2, :], axis=-1, keepdims=True))
           - jnp.exp(jnp.sum(lf[2:3, :] * lf[3:4, :], axis=-1, keepdims=True)) + lam_init)
    lane = lax.broadcasted_iota(jnp.int32, (tq, LANES), 1)

    def scores(i, s_ref):
        q = q_ref[pl.ds(pl.multiple_of(i * tq, tq), tq), :]
        kbf = k_ref[...]
        mx = []
        for c in range(2):
            keep = (lane < DA_HEAD_DIM) if c == 0 else (lane >= DA_HEAD_DIM)
            s = lax.dot_general(kbf, jnp.where(keep, q, jnp.zeros_like(q)), NT_DIMS,
                                preferred_element_type=F32)
            s_ref[c] = s
            mx.append(jnp.max(s, axis=0, keepdims=True))
        return tuple(mx)

    def attend(i, s_ref, mx):
        outs = []
        for c in range(2):
            p = jnp.exp2(s_ref[c] - mx[c])
            inv = 1.0 / jnp.sum(p, axis=0, keepdims=True)
            outs.append(jnp.dot(vt_ref[...], p.astype(BF16), preferred_element_type=F32) * inv)
        o = (outs[0] - outs[1] * lam).T
        y = o * lax.rsqrt(jnp.mean(o * o, axis=-1, keepdims=True) + RMS_EPS)
        o_ref[pl.ds(pl.multiple_of(i * tq, tq), tq), :] = (
            y * g_ref[...] * (1.0 - lam_init)).astype(o_ref.dtype)

    m_a = scores(0, sa_ref)

    def pair(t, m_a):
        i = 2 * t
        m_b = scores(i + 1, sb_ref)
        attend(i, sa_ref, m_a)
        m_a = scores(i + 2, sa_ref)
        attend(i + 1, sb_ref, m_b)
        return m_a

    m_a = lax.fori_loop(0, nq // 2 - 1, pair, m_a)
    m_b = scores(nq - 1, sb_ref)
    attend(nq - 2, sa_ref, m_a)
    attend(nq - 1, sb_ref, m_b)


def _diff_attn(proj3, da_lambda, da_norm_g, layer, tq, kb):
    B, S, _ = proj3.shape
    lam_init = 0.8 - 0.6 * math.exp(-0.3 * layer)
    return pl.pallas_call(
        functools.partial(_attn_kernel, lam_init=lam_init, tq=tq, kb=kb),
        grid=(B, DA_HEADS),
        in_specs=[pl.BlockSpec((4, DA_HEAD_DIM), lambda b, h: (0, 0)),
                  pl.BlockSpec((1, LANES), lambda b, h: (0, 0)),
                  pl.BlockSpec((None, S, LANES), lambda b, h: (b, 0, h)),
                  pl.BlockSpec((None, S, LANES), lambda b, h: (b, 0, 4 + h)),
                  pl.BlockSpec((None, S, LANES), lambda b, h: (b, 0, 8 + h))],
        out_specs=pl.BlockSpec((None, S, LANES), lambda b, h: (b, 0, h)),
        out_shape=jax.ShapeDtypeStruct((B, S, DA_HEADS * LANES), BF16),
        scratch_shapes=[pltpu.VMEM((LANES, S), BF16),
                        pltpu.VMEM((2, S, tq), F32),
                        pltpu.VMEM((2, S, tq), F32)],
        compiler_params=_cparams(("parallel", "parallel")),
        name="diff_attn",
    )(da_lambda, da_norm_g.reshape(1, LANES), proj3, proj3, proj3)


def _bcast_rows(b, idxs, rep):
    return jnp.concatenate(
        [jnp.broadcast_to(b[r:r + 1, :], (rep, b.shape[1])) for r in idxs], axis=0)


def _ref_rows(b, m, reverse, row):
    C = b.shape[0]
    off = m if reverse else m - 1
    if m >= 4:
        return _bcast_rows(b, [G * 2 * m + off for G in range(C // (2 * m))], 2 * m)
    lo = _bcast_rows(b, [8 * G + off for G in range(C // 8)], 8)
    hi = _bcast_rows(b, [8 * G + 4 + off for G in range(C // 8)], 8)
    return jnp.where((row & 7) < 4, lo, hi)


def _hgrn_chunk(q, g, v, state_t, tri, reverse):
    C = q.shape[0]
    row = lax.broadcasted_iota(jnp.int32, (C, C), 0)
    col = lax.broadcasted_iota(jnp.int32, (C, C), 1)
    f = jnp.exp(g)
    k = 1.0 - f

    g1 = g.astype(BF16)
    r1 = g - g1.astype(F32)
    g2 = r1.astype(BF16)
    g3 = (r1 - g2.astype(F32)).astype(BF16)
    cs = jnp.dot(tri, jnp.concatenate([g1, g2, g3], axis=1), preferred_element_type=F32)
    b = cs[:, :LANES] + cs[:, LANES:2 * LANES] + cs[:, 2 * LANES:]

    edge = 0 if reverse else C - 1
    b_edge = jnp.broadcast_to(b[edge:edge + 1, :], b.shape)
    q_hat = q * jnp.exp(b)
    k_hat = k * jnp.exp(b_edge - b)

    attn = jnp.where(row == col,
                     lax.dot_general(q.astype(BF16), k.astype(BF16), NT_DIMS,
                                     preferred_element_type=F32), 0.0)
    m = 1
    while m < C:
        shift = m.bit_length() - 1
        is_query = ((row >> shift) & 1) == (0 if reverse else 1)
        if m == 1:
            qt = jnp.where(is_query, q * f, 0.0)
            kt = jnp.where(is_query, 0.0, k)
        else:
            ref = _ref_rows(b, m, reverse, row)
            e = jnp.exp(jnp.where(is_query, b - ref, ref - b))
            qt = jnp.where(is_query, q * e, 0.0)
            kt = jnp.where(is_query, 0.0, k * e)
        s = lax.dot_general(qt.astype(BF16), kt.astype(BF16), NT_DIMS, preferred_element_type=F32)
        attn = attn + jnp.where((row >> (shift + 1)) == (col >> (shift + 1)), s, 0.0)
        m *= 2

    o = (jnp.dot(attn.astype(BF16), v.astype(BF16), preferred_element_type=F32)
         + lax.dot_general(q_hat.astype(BF16), state_t.astype(BF16), NT_DIMS,
                           preferred_element_type=F32))
    upd = lax.dot_general(v.astype(BF16), k_hat.astype(BF16), TN_DIMS, preferred_element_type=F32)
    new_state = state_t * jnp.exp(b[edge:edge + 1, :]) + upd
    return o, new_state


def _hgrn_kernel(q_ref, gf_ref, gb_ref, v_ref, of_ref, ob_ref):
    S = q_ref.shape[0]
    C = HG_CHUNK
    n = S // C
    row = lax.broadcasted_iota(jnp.int32, (C, C), 0)
    col = lax.broadcasted_iota(jnp.int32, (C, C), 1)
    tri_f = jnp.where(col <= row, 1.0, 0.0).astype(BF16)
    tri_b = jnp.where(col >= row, 1.0, 0.0).astype(BF16)

    def body(c, carry):
        st_f, st_b = carry
        sf = pl.multiple_of(c * C, C)
        sb = pl.multiple_of((n - 1 - c) * C, C)
        o_f, st_f = _hgrn_chunk(q_ref[pl.ds(sf, C), :].astype(F32), gf_ref[pl.ds(sf, C), :],
                                v_ref[pl.ds(sf, C), :], st_f, tri_f, False)
        of_ref[pl.ds(sf, C), :] = o_f
        o_b, st_b = _hgrn_chunk(q_ref[pl.ds(sb, C), :].astype(F32), gb_ref[pl.ds(sb, C), :],
                                v_ref[pl.ds(sb, C), :], st_b, tri_b, True)
        ob_ref[pl.ds(sb, C), :] = o_b
        return st_f, st_b

    z = jnp.zeros((LANES, LANES), F32)
    lax.fori_loop(0, n, body, (z, z))


def _hgrn(pa3, pg3, pc3):
    B, S, _ = pa3.shape
    spec = lambda blk: pl.BlockSpec((None, S, LANES), lambda b, h: (b, 0, blk + h))
    out = jax.ShapeDtypeStruct((B, S, HG_HEADS * LANES), F32)
    return pl.pallas_call(
        _hgrn_kernel,
        grid=(B, HG_HEADS),
        in_specs=[spec(12), spec(0), spec(4), spec(0)],
        out_specs=[pl.BlockSpec((None, S, LANES), lambda b, h: (b, 0, h)),
                   pl.BlockSpec((None, S, LANES), lambda b, h: (b, 0, h))],
        out_shape=[out, out],
        compiler_params=_cparams(("parallel", "parallel")),
        name="hgrn2",
    )(pa3, pg3, pg3, pc3)


def _mm_kernel(x_ref, w_ref, o_ref):
    o_ref[...] = jnp.dot(x_ref[...].astype(BF16), w_ref[...],
                         preferred_element_type=F32).astype(o_ref.dtype)


def _matmul(x, w, tm, tn):
    M, K = x.shape
    N = w.shape[1]
    return pl.pallas_call(
        _mm_kernel,
        grid=(M // tm, N // tn),
        in_specs=[pl.BlockSpec((tm, K), lambda i, j: (i, 0)),
                  pl.BlockSpec((K, tn), lambda i, j: (0, j))],
        out_specs=pl.BlockSpec((tm, tn), lambda i, j: (i, j)),
        out_shape=jax.ShapeDtypeStruct((M, N), BF16),
        compiler_params=_cparams(("parallel", "parallel")),
        name="mem_kv_proj",
    )(x, w)


def _outproj_kernel(da_ref, of_ref, ob_ref, gate_ref, hgn_ref, mq_ref, mkv_ref, w_ref, x_ref,
                    g_ref, b_ref, o_ref, obf_ref):
    hg = of_ref[...] + ob_ref[...]
    gate = gate_ref[...].astype(F32)
    parts = [da_ref[...]]
    for h in range(HG_HEADS):
        sl = slice(h * LANES, (h + 1) * LANES)
        y = hg[:, sl]
        y = y * lax.rsqrt(jnp.mean(y * y, axis=-1, keepdims=True) + RMS_EPS) * hgn_ref[...]
        parts.append((y * gate[:, sl]).astype(BF16))
    width = MX_HEADS * LANES
    for h in range(MX_HEADS):
        sl = slice(h * LANES, (h + 1) * LANES)
        s = lax.dot_general(mq_ref[:, sl], mkv_ref[:, sl], NT_DIMS, preferred_element_type=F32)
        p = jnp.exp(s - jnp.max(s, axis=-1, keepdims=True))
        p = p * (1.0 / jnp.sum(p, axis=-1, keepdims=True))
        mv = mkv_ref[:, width + h * LANES:width + (h + 1) * LANES]
        parts.append(jnp.dot(p.astype(BF16), mv, preferred_element_type=F32).astype(BF16))
    mix = jnp.dot(jnp.concatenate(parts, axis=1), w_ref[...], preferred_element_type=F32)
    y = _layer_norm_rows(DEEPNORM_ALPHA * x_ref[...] + mix, g_ref[...], b_ref[...])
    o_ref[...] = y
    obf_ref[...] = y.astype(BF16)


def _outproj(da_o, o_f, o_b, pc, hg_norm_g, mkv3, w_out, x, g, b, tm, seq):
    T, D = x.shape
    W = SEC
    M = mkv3.shape[1]
    per_seq = seq // tm
    row = lambda i: (i, 0)
    const = lambda i: (0, 0)
    return pl.pallas_call(
        _outproj_kernel,
        grid=(T // tm,),
        in_specs=[pl.BlockSpec((tm, W), row), pl.BlockSpec((tm, W), row), pl.BlockSpec((tm, W), row),
                  pl.BlockSpec((tm, W), lambda i: (i, 1)),
                  pl.BlockSpec((1, LANES), const),
                  pl.BlockSpec((tm, W), lambda i: (i, 2)),
                  pl.BlockSpec((None, M, 2 * W), lambda i: (i // per_seq, 0, 0)),
                  pl.BlockSpec((3 * W, D), const),
                  pl.BlockSpec((tm, D), row),
                  pl.BlockSpec((1, D), const), pl.BlockSpec((1, D), const)],
        out_specs=[pl.BlockSpec((tm, D), row), pl.BlockSpec((tm, D), row)],
        out_shape=[jax.ShapeDtypeStruct((T, D), F32), jax.ShapeDtypeStruct((T, D), BF16)],
        compiler_params=_cparams(("parallel",)),
        name="outproj_ln",
    )(da_o, o_f, o_b, pc, hg_norm_g.reshape(1, LANES), pc, mkv3, w_out, x, g.reshape(1, D),
      b.reshape(1, D))


FF_TILE = 256
HALO = 16


def _ffn_kernel(xp_ref, xm_ref, xn_ref, xr_ref, wu_ref, cw_ref, cb_ref, wd_ref, g_ref, b_ref,
                o_ref, obf_ref, xcat_ref, u_ref, *, tiles_per_seq):
    i = pl.program_id(0)
    tm = xm_ref.shape[0]
    first = (i % tiles_per_seq) == 0
    last = (i % tiles_per_seq) == tiles_per_seq - 1
    xcat_ref[0:HALO, :] = jnp.where(first, jnp.zeros_like(xp_ref[...]), xp_ref[...])
    xcat_ref[HALO:HALO + tm, :] = xm_ref[...]
    xcat_ref[HALO + tm:, :] = jnp.where(last, jnp.zeros_like(xn_ref[...]), xn_ref[...])
    xc = xcat_ref[...]

    acc = None
    for j in range(D_FF // FF_TILE):
        slot = j % 2
        halves = []
        for part, base in enumerate((0, D_FF)):
            cols = slice(base + j * FF_TILE, base + (j + 1) * FF_TILE)
            u_ref[slot, part] = jnp.dot(xc, wu_ref[:, cols], preferred_element_type=F32)
            cw = cw_ref[:, cols]
            halves.append(u_ref[slot, part, pl.ds(HALO - 1, tm), :] * cw[0:1, :]
                          + u_ref[slot, part, pl.ds(HALO, tm), :] * cw[1:2, :]
                          + u_ref[slot, part, pl.ds(HALO + 1, tm), :] * cw[2:3, :]
                          + cb_ref[:, cols])
        gate, val = halves
        act = (gate * _sigmoid(gate) * val).astype(BF16)
        part = jnp.dot(act, wd_ref[j * FF_TILE:(j + 1) * FF_TILE, :], preferred_element_type=F32)
        acc = part if acc is None else acc + part

    y = _layer_norm_rows(DEEPNORM_ALPHA * xr_ref[...] + acc, g_ref[...], b_ref[...])
    o_ref[...] = y
    obf_ref[...] = y.astype(BF16)


def _ffn(xb, x, w_up, conv_w, conv_b, w_down, g, b, tm, seq):
    T, D = x.shape
    hb = tm // HALO
    nhb = T // HALO
    row = lambda i: (i, 0)
    const = lambda i: (0, 0)
    resident = lambda shape: pl.BlockSpec(shape, const, pipeline_mode=pl.Buffered(1))
    return pl.pallas_call(
        functools.partial(_ffn_kernel, tiles_per_seq=seq // tm),
        grid=(T // tm,),
        in_specs=[pl.BlockSpec((HALO, D), lambda i: (jnp.maximum(i * hb - 1, 0), 0)),
                  pl.BlockSpec((tm, D), row),
                  pl.BlockSpec((HALO, D), lambda i: (jnp.minimum((i + 1) * hb, nhb - 1), 0)),
                  pl.BlockSpec((tm, D), row),
                  resident((D, 2 * D_FF)),
                  resident((3, 2 * D_FF)),
                  resident((1, 2 * D_FF)),
                  resident((D_FF, D)),
                  pl.BlockSpec((1, D), const), pl.BlockSpec((1, D), const)],
        out_specs=[pl.BlockSpec((tm, D), row), pl.BlockSpec((tm, D), row)],
        out_shape=[jax.ShapeDtypeStruct((T, D), F32), jax.ShapeDtypeStruct((T, D), BF16)],
        scratch_shapes=[pltpu.VMEM((tm + 2 * HALO, D), BF16),
                        pltpu.VMEM((2, 2, tm + 2 * HALO, FF_TILE), F32)],
        compiler_params=_cparams(("parallel",)),
        name="conv_ffn_ln",
    )(xb, xb, xb, x, w_up, conv_w, conv_b.reshape(1, -1), w_down, g.reshape(1, D), b.reshape(1, D))


def _rope_tables(positions):
    half = ROPE_DIM // 2
    inv_freq = ROPE_THETA ** (-jnp.arange(0, ROPE_DIM, 2, dtype=F32) / ROPE_DIM)
    ang = positions.astype(F32)[..., None] * inv_freq
    cos, sin = jnp.cos(ang), jnp.sin(ang)
    pad = DA_HEAD_DIM - ROPE_DIM
    shape = cos.shape[:-1] + (pad,)
    cos64 = jnp.concatenate([cos, cos, jnp.ones(shape, F32)], axis=-1)
    sin64 = jnp.concatenate([-sin, sin, jnp.zeros(shape, F32)], axis=-1)
    cosf = jnp.concatenate([cos64, cos64], axis=-1)
    sinf = jnp.concatenate([sin64, sin64], axis=-1)
    T = cosf.shape[0] * cosf.shape[1]
    return cosf.reshape(T, LANES), sinf.reshape(T, LANES)


def _pick(n, target):
    t = min(n, target)
    while n % t:
        t //= 2
    return t


def kernel(x, mem, positions, ln_in_g, ln_in_b, w_in, da_lambda, da_norm_g, hg_lb_fwd, hg_lb_bwd,
           hg_norm_g, w_mem_kv, w_out, ln1_g, ln1_b, w_up, conv_w, conv_b, w_down, ln2_g, ln2_b):
    B, S, D = x.shape
    M = mem.shape[1]
    T = B * S
    tm = _pick(S, 1024)
    cosf, sinf = _rope_tables(positions)
    w_in_b, w_kv_b, w_out_b = w_in.astype(BF16), w_mem_kv.astype(BF16), w_out.astype(BF16)
    w_up_b, w_down_b = w_up.astype(BF16), w_down.astype(BF16)
    mem2 = mem.reshape(B * M, D)

    h, hb = _ln_in(x.reshape(T, D), ln_in_g, ln_in_b, tm)
    for l in range(DEPTH):
        pa, pg, pc = _inproj(hb, w_in_b[l], cosf, sinf, hg_lb_fwd, hg_lb_bwd, l, tm)
        pa3 = pa.reshape(B, S, 4 * SEC)
        da_o = _diff_attn(pa3, da_lambda[l], da_norm_g[l], l, _pick(S, 256), _pick(S, 512))
        o_f, o_b = _hgrn(pa3, pg.reshape(B, S, 2 * SEC), pc.reshape(B, S, 3 * SEC))
        mkv = _matmul(mem2, w_kv_b[l], _pick(B * M, 512), 512)
        h, hb = _outproj(da_o.reshape(T, SEC), o_f.reshape(T, SEC), o_b.reshape(T, SEC), pc,
                         hg_norm_g[l], mkv.reshape(B, M, 2 * SEC), w_out_b[l], h, ln1_g[l],
                         ln1_b[l], _pick(S, 512), S)
        h, hb = _ffn(hb, h, w_up_b[l], conv_w[l], conv_b[l], w_down_b[l], ln2_g[l], ln2_b[l],
                     _pick(S, 512), S)
    return h.reshape(B, S, D)
```

```python
import functools
import math

import jax
import jax.numpy as jnp
from jax import lax
from jax.experimental import pallas as pl
from jax.experimental.pallas import tpu as pltpu

F32 = jnp.float32
BF16 = jnp.bfloat16

D_MODEL = 1024
DEPTH = 4
DA_HEADS = 4
DA_HEAD_DIM = 64
HG_HEADS = 4
HG_MIN_FORGET = 1e-20
MX_HEADS = 4
MX_HEAD_DIM = 128
ROPE_THETA = 500000.0
ROPE_DIM = DA_HEAD_DIM // 4
D_FF = 2816
LN_EPS = 1e-5
RMS_EPS = 1e-6
DEEPNORM_ALPHA = (2 * DEPTH) ** 0.25
LOG2E = 1.4426950408889634

LANES = 128
SEC = 512
N_SEC = 9
HG_CHUNK = 128
VMEM_LIMIT = 56 * 1024 * 1024

NT_DIMS = (((1,), (1,)), ((), ()))
TN_DIMS = (((0,), (0,)), ((), ()))


def _cparams(sem):
    return pltpu.CompilerParams(dimension_semantics=sem, vmem_limit_bytes=VMEM_LIMIT)


def _layer_norm_rows(y, g, b):
    mu = jnp.mean(y, axis=-1, keepdims=True)
    d = y - mu
    var = jnp.mean(d * d, axis=-1, keepdims=True)
    return d * lax.rsqrt(var + LN_EPS) * g + b


def _sigmoid(z):
    return 1.0 / (1.0 + jnp.exp(-z))


def _ln_kernel(x_ref, g_ref, b_ref, o_ref, ob_ref):
    y = _layer_norm_rows(x_ref[...], g_ref[...], b_ref[...])
    o_ref[...] = y
    ob_ref[...] = y.astype(BF16)


def _ln_in(x2, g, b, tm):
    T, D = x2.shape
    return pl.pallas_call(
        _ln_kernel,
        grid=(T // tm,),
        in_specs=[pl.BlockSpec((tm, D), lambda i: (i, 0)),
                  pl.BlockSpec((1, D), lambda i: (0, 0)),
                  pl.BlockSpec((1, D), lambda i: (0, 0))],
        out_specs=[pl.BlockSpec((tm, D), lambda i: (i, 0)),
                   pl.BlockSpec((tm, D), lambda i: (i, 0))],
        out_shape=[jax.ShapeDtypeStruct((T, D), F32), jax.ShapeDtypeStruct((T, D), BF16)],
        compiler_params=_cparams(("parallel",)),
        name="ln_in",
    )(x2, g.reshape(1, D), b.reshape(1, D))


def _lower_bound(lb_ref, layer):
    z = lb_ref[...]
    e = jnp.exp(z - jnp.max(z, axis=0, keepdims=True))
    p = e / jnp.sum(e, axis=0, keepdims=True)
    lb = jnp.zeros((1, z.shape[1]), F32)
    for j in range(1, layer + 1):
        lb = lb + p[j:j + 1, :]
    return lb


def _inproj_kernel(x_ref, w_ref, cos_ref, sin_ref, lbf_ref, lbb_ref, a_ref, g_ref, c_ref,
                   *, layer, sub):
    n = pl.program_id(1)
    tm = x_ref.shape[0]

    def run(epilogue, out_ref):
        for r in range(tm // sub):
            rows = slice(r * sub, (r + 1) * sub)
            acc = jnp.dot(x_ref[rows, :], w_ref[...], preferred_element_type=F32)
            out_ref[rows, :] = epilogue(acc, rows).astype(out_ref.dtype)

    def rope(scale):
        def epilogue(acc, rows):
            cs = cos_ref[rows, :]
            sn = sin_ref[rows, :]
            lane = lax.broadcasted_iota(jnp.int32, cs.shape, 1)
            first = (lane & 15) < 8
            parts = []
            for j in range(SEC // LANES):
                t = acc[:, j * LANES:(j + 1) * LANES]
                partner = jnp.where(first, pltpu.roll(t, LANES - 8, axis=1), pltpu.roll(t, 8, axis=1))
                parts.append((t * cs + partner * sn) * scale)
            return jnp.concatenate(parts, axis=1)
        return epilogue

    def log_forget(lb_ref):
        def epilogue(acc, rows):
            lb = _lower_bound(lb_ref, layer)
            f = lb + (1.0 - lb) * _sigmoid(acc)
            return jnp.log(jnp.maximum(f, HG_MIN_FORGET))
        return epilogue

    identity = lambda acc, rows: acc
    silu = lambda acc, rows: acc * _sigmoid(acc)
    branches = [
        (rope(DA_HEAD_DIM ** -0.5 * LOG2E), a_ref),
        (rope(1.0), a_ref),
        (identity, a_ref),
        (silu, a_ref),
        (log_forget(lbf_ref), g_ref),
        (log_forget(lbb_ref), g_ref),
        (identity, c_ref),
        (silu, c_ref),
        (lambda acc, rows: acc * (MX_HEAD_DIM ** -0.5), c_ref),
    ]
    for k, (epilogue, out_ref) in enumerate(branches):
        pl.when(n == k)(functools.partial(run, epilogue, out_ref))


def _inproj(xb, w, cosf, sinf, lbf, lbb, layer, tm):
    T, D = xb.shape
    return pl.pallas_call(
        functools.partial(_inproj_kernel, layer=layer, sub=min(tm, 256)),
        grid=(T // tm, N_SEC),
        in_specs=[pl.BlockSpec((tm, D), lambda i, n: (i, 0)),
                  pl.BlockSpec((D, SEC), lambda i, n: (0, n)),
                  pl.BlockSpec((tm, LANES), lambda i, n: (i, 0)),
                  pl.BlockSpec((tm, LANES), lambda i, n: (i, 0)),
                  pl.BlockSpec((DEPTH, SEC), lambda i, n: (0, 0)),
                  pl.BlockSpec((DEPTH, SEC), lambda i, n: (0, 0))],
        out_specs=[pl.BlockSpec((tm, SEC), lambda i, n: (i, jnp.minimum(n, 3))),
                   pl.BlockSpec((tm, SEC), lambda i, n: (i, jnp.clip(n - 4, 0, 1))),
                   pl.BlockSpec((tm, SEC), lambda i, n: (i, jnp.clip(n - 6, 0, 2)))],
        out_shape=[jax.ShapeDtypeStruct((T, 4 * SEC), BF16),
                   jax.ShapeDtypeStruct((T, 2 * SEC), F32),
                   jax.ShapeDtypeStruct((T, 3 * SEC), BF16)],
        compiler_params=_cparams(("parallel", "arbitrary")),
        name="inproj",
    )(xb, w, cosf, sinf, lbf, lbb)


def _attn_kernel(lam_ref, g_ref, q_ref, k_ref, v_ref, o_ref, vt_ref, sa_ref, sb_ref,
                 *, lam_init, tq, kb):
    S = k_ref.shape[0]
    nq = S // tq
    for j in range(S // kb):
        vt_ref[:, j * kb:(j + 1) * kb] = v_ref[j * kb:(j + 1) * kb, :].astype(F32).T.astype(BF16)

    lf = lam_ref[...]
    lam = (jnp.exp(jnp.sum(lf[0:1, :] * lf[1:2, :], axis=-1, keepdims=True))
           - jnp.exp(jnp.sum(lf[2:3, :] * lf[3:4, :], axis=-1, keepdims=True)) + lam_init)
    lane = lax.broadcasted_iota(jnp.int32, (tq, LANES), 1)

    def scores(i, s_ref):
        q = q_ref[pl.ds(pl.multiple_of(i * tq, tq), tq), :]
        kbf = k_ref[...]
        mx = []
        for c in range(2):
            keep = (lane < DA_HEAD_DIM) if c == 0 else (lane >= DA_HEAD_DIM)
            s = lax.dot_general(kbf, jnp.where(keep, q, jnp.zeros_like(q)), NT_DIMS,
                                preferred_element_type=F32)
            s_ref[c] = s
            mx.append(jnp.max(s, axis=0, keepdims=True))
        return tuple(mx)

    def attend(i, s_ref, mx):
        outs = []
        for c in range(2):
            p = jnp.exp2(s_ref[c] - mx[c])
            inv = 1.0 / jnp.sum(p, axis=0, keepdims=True)
            outs.append(jnp.dot(vt_ref[...], p.astype(BF16), preferred_element_type=F32) * inv)
        o = (outs[0] - outs[1] * lam).T
        y = o * lax.rsqrt(jnp.mean(o * o, axis=-1, keepdims=True) + RMS_EPS)
        o_ref[pl.ds(pl.multiple_of(i * tq, tq), tq), :] = (
            y * g_ref[...] * (1.0 - lam_init)).astype(o_ref.dtype)

    m_a = scores(0, sa_ref)

    def pair(t, m_a):
        i = 2 * t
        m_b = scores(i + 1, sb_ref)
        attend(i, sa_ref, m_a)
        m_a = scores(i + 2, sa_ref)
        attend(i + 1, sb_ref, m_b)
        return m_a

    m_a = lax.fori_loop(0, nq // 2 - 1, pair, m_a)
    m_b = scores(nq - 1, sb_ref)
    attend(nq - 2, sa_ref, m_a)
    attend(nq - 1, sb_ref, m_b)


def _diff_attn(proj3, da_lambda, da_norm_g, layer, tq, kb):
    B, S, _ = proj3.shape
    lam_init = 0.8 - 0.6 * math.exp(-0.3 * layer)
    return pl.pallas_call(
        functools.partial(_attn_kernel, lam_init=lam_init, tq=tq, kb=kb),
        grid=(B, DA_HEADS),
        in_specs=[pl.BlockSpec((4, DA_HEAD_DIM), lambda b, h: (0, 0)),
                  pl.BlockSpec((1, LANES), lambda b, h: (0, 0)),
                  pl.BlockSpec((None, S, LANES), lambda b, h: (b, 0, h)),
                  pl.BlockSpec((None, S, LANES), lambda b, h: (b, 0, 4 + h)),
                  pl.BlockSpec((None, S, LANES), lambda b, h: (b, 0, 8 + h))],
        out_specs=pl.BlockSpec((None, S, LANES), lambda b, h: (b, 0, h)),
        out_shape=jax.ShapeDtypeStruct((B, S, DA_HEADS * LANES), BF16),
        scratch_shapes=[pltpu.VMEM((LANES, S), BF16),
                        pltpu.VMEM((2, S, tq), F32),
                        pltpu.VMEM((2, S, tq), F32)],
        compiler_params=_cparams(("parallel", "parallel")),
        name="diff_attn",
    )(da_lambda, da_norm_g.reshape(1, LANES), proj3, proj3, proj3)


def _bcast_rows(b, idxs, rep):
    return jnp.concatenate(
        [jnp.broadcast_to(b[r:r + 1, :], (rep, b.shape[1])) for r in idxs], axis=0)


def _ref_rows(b, m, reverse, row):
    C = b.shape[0]
    off = m if reverse else m - 1
    if m >= 4:
        return _bcast_rows(b, [G * 2 * m + off for G in range(C // (2 * m))], 2 * m)
    lo = _bcast_rows(b, [8 * G + off for G in range(C // 8)], 8)
    hi = _bcast_rows(b, [8 * G + 4 + off for G in range(C // 8)], 8)
    return jnp.where((row & 7) < 4, lo, hi)


def _level_masks(C, reverse):
    row = lax.broadcasted_iota(jnp.int32, (C, C), 0)
    col = lax.broadcasted_iota(jnp.int32, (C, C), 1)
    masks = [jnp.where(row == col, 1.0, 0.0)]
    m = 1
    while m < C:
        shift = m.bit_length() - 1
        q_bit, k_bit = (0, 1) if reverse else (1, 0)
        keep = (((row >> (shift + 1)) == (col >> (shift + 1)))
                & (((row >> shift) & 1) == q_bit) & (((col >> shift) & 1) == k_bit))
        masks.append(jnp.where(keep, 1.0, 0.0))
        m *= 2
    return masks


def _hgrn_chunk(q, g, v, state_t, tri, masks, reverse):
    C = q.shape[0]
    row = lax.broadcasted_iota(jnp.int32, (C, LANES), 0)
    f = jnp.exp(g)
    k = (1.0 - f).astype(BF16)

    g1 = g.astype(BF16)
    r1 = g - g1.astype(F32)
    g2 = r1.astype(BF16)
    g3 = (r1 - g2.astype(F32)).astype(BF16)
    cs = jnp.dot(tri, jnp.concatenate([g1, g2, g3], axis=1), preferred_element_type=F32)
    b = cs[:, :LANES] + cs[:, LANES:2 * LANES] + cs[:, 2 * LANES:]

    edge = 0 if reverse else C - 1
    b_edge = b[edge:edge + 1, :]
    q_hat = q * jnp.exp2(b * LOG2E).astype(BF16)
    k_hat = k * jnp.exp2((b_edge - b) * LOG2E).astype(BF16)

    attn = lax.dot_general(q, k, NT_DIMS, preferred_element_type=F32) * masks[0]
    m = 1
    while m < C:
        shift = m.bit_length() - 1
        if m == 1:
            is_query = (row & 1) == (0 if reverse else 1)
            e = jnp.where(is_query, f, 1.0).astype(BF16)
        else:
            d = b - _ref_rows(b, m, reverse, row)
            e = jnp.exp2(jnp.abs(d) * (-LOG2E)).astype(BF16)
        s = lax.dot_general(q * e, k * e, NT_DIMS, preferred_element_type=F32)
        attn = attn + s * masks[shift + 1]
        m *= 2

    o = (jnp.dot(attn.astype(BF16), v, preferred_element_type=F32)
         + lax.dot_general(q_hat, state_t.astype(BF16), NT_DIMS, preferred_element_type=F32))
    upd = lax.dot_general(v, k_hat, TN_DIMS, preferred_element_type=F32)
    new_state = state_t * jnp.exp(b_edge) + upd
    return o, new_state


def _hgrn_kernel(q_ref, gf_ref, gb_ref, v_ref, of_ref, ob_ref):
    S = q_ref.shape[0]
    C = HG_CHUNK
    n = S // C
    row = lax.broadcasted_iota(jnp.int32, (C, C), 0)
    col = lax.broadcasted_iota(jnp.int32, (C, C), 1)
    tri_f = jnp.where(col <= row, 1.0, 0.0).astype(BF16)
    tri_b = jnp.where(col >= row, 1.0, 0.0).astype(BF16)
    masks_f = _level_masks(C, False)
    masks_b = _level_masks(C, True)

    def body(c, carry):
        st_f, st_b = carry
        sf = pl.multiple_of(c * C, C)
        sb = pl.multiple_of((n - 1 - c) * C, C)
        o_f, st_f = _hgrn_chunk(q_ref[pl.ds(sf, C), :], gf_ref[pl.ds(sf, C), :],
                                v_ref[pl.ds(sf, C), :], st_f, tri_f, masks_f, False)
        of_ref[pl.ds(sf, C), :] = o_f
        o_b, st_b = _hgrn_chunk(q_ref[pl.ds(sb, C), :], gb_ref[pl.ds(sb, C), :],
                                v_ref[pl.ds(sb, C), :], st_b, tri_b, masks_b, True)
        ob_ref[pl.ds(sb, C), :] = o_b
        return st_f, st_b

    z = jnp.zeros((LANES, LANES), F32)
    lax.fori_loop(0, n, body, (z, z), unroll=2)


def _hgrn(pa3, pg3, pc3):
    B, S, _ = pa3.shape
    spec = lambda blk: pl.BlockSpec((None, S, LANES), lambda b, h: (b, 0, blk + h))
    out = jax.ShapeDtypeStruct((B, S, HG_HEADS * LANES), F32)
    return pl.pallas_call(
        _hgrn_kernel,
        grid=(B, HG_HEADS),
        in_specs=[spec(12), spec(0), spec(4), spec(0)],
        out_specs=[pl.BlockSpec((None, S, LANES), lambda b, h: (b, 0, h)),
                   pl.BlockSpec((None, S, LANES), lambda b, h: (b, 0, h))],
        out_shape=[out, out],
        compiler_params=_cparams(("parallel", "parallel")),
        name="hgrn2",
    )(pa3, pg3, pg3, pc3)


def _mm_kernel(x_ref, w_ref, o_ref):
    o_ref[...] = jnp.dot(x_ref[...].astype(BF16), w_ref[...],
                         preferred_element_type=F32).astype(o_ref.dtype)


def _matmul(x, w, tm, tn):
    M, K = x.shape
    N = w.shape[1]
    return pl.pallas_call(
        _mm_kernel,
        grid=(M // tm, N // tn),
        in_specs=[pl.BlockSpec((tm, K), lambda i, j: (i, 0)),
                  pl.BlockSpec((K, tn), lambda i, j: (0, j))],
        out_specs=pl.BlockSpec((tm, tn), lambda i, j: (i, j)),
        out_shape=jax.ShapeDtypeStruct((M, N), BF16),
        compiler_params=_cparams(("parallel", "parallel")),
        name="mem_kv_proj",
    )(x, w)


def _outproj_kernel(da_ref, of_ref, ob_ref, gate_ref, hgn_ref, mq_ref, mkv_ref, w_ref, x_ref,
                    g_ref, b_ref, o_ref, obf_ref):
    hg = of_ref[...] + ob_ref[...]
    gate = gate_ref[...].astype(F32)
    parts = [da_ref[...]]
    for h in range(HG_HEADS):
        sl = slice(h * LANES, (h + 1) * LANES)
        y = hg[:, sl]
        y = y * lax.rsqrt(jnp.mean(y * y, axis=-1, keepdims=True) + RMS_EPS) * hgn_ref[...]
        parts.append((y * gate[:, sl]).astype(BF16))
    width = MX_HEADS * LANES
    for h in range(MX_HEADS):
        sl = slice(h * LANES, (h + 1) * LANES)
        s = lax.dot_general(mq_ref[:, sl], mkv_ref[:, sl], NT_DIMS, preferred_element_type=F32)
        p = jnp.exp(s - jnp.max(s, axis=-1, keepdims=True))
        p = p * (1.0 / jnp.sum(p, axis=-1, keepdims=True))
        mv = mkv_ref[:, width + h * LANES:width + (h + 1) * LANES]
        parts.append(jnp.dot(p.astype(BF16), mv, preferred_element_type=F32).astype(BF16))
    mix = jnp.dot(jnp.concatenate(parts, axis=1), w_ref[...], preferred_element_type=F32)
    y = _layer_norm_rows(DEEPNORM_ALPHA * x_ref[...] + mix, g_ref[...], b_ref[...])
    o_ref[...] = y
    obf_ref[...] = y.astype(BF16)


def _outproj(da_o, o_f, o_b, pc, hg_norm_g, mkv3, w_out, x, g, b, tm, seq):
    T, D = x.shape
    W = SEC
    M = mkv3.shape[1]
    per_seq = seq // tm
    row = lambda i: (i, 0)
    const = lambda i: (0, 0)
    return pl.pallas_call(
        _outproj_kernel,
        grid=(T // tm,),
        in_specs=[pl.BlockSpec((tm, W), row), pl.BlockSpec((tm, W), row), pl.BlockSpec((tm, W), row),
                  pl.BlockSpec((tm, W), lambda i: (i, 1)),
                  pl.BlockSpec((1, LANES), const),
                  pl.BlockSpec((tm, W), lambda i: (i, 2)),
                  pl.BlockSpec((None, M, 2 * W), lambda i: (i // per_seq, 0, 0)),
                  pl.BlockSpec((3 * W, D), const),
                  pl.BlockSpec((tm, D), row),
                  pl.BlockSpec((1, D), const), pl.BlockSpec((1, D), const)],
        out_specs=[pl.BlockSpec((tm, D), row), pl.BlockSpec((tm, D), row)],
        out_shape=[jax.ShapeDtypeStruct((T, D), F32), jax.ShapeDtypeStruct((T, D), BF16)],
        compiler_params=_cparams(("parallel",)),
        name="outproj_ln",
    )(da_o, o_f, o_b, pc, hg_norm_g.reshape(1, LANES), pc, mkv3, w_out, x, g.reshape(1, D),
      b.reshape(1, D))


FF_TILE = 256
HALO = 16


def _ffn_kernel(xp_ref, xm_ref, xn_ref, xr_ref, wu_ref, cw_ref, cb_ref, wd_ref, g_ref, b_ref,
                o_ref, obf_ref, xcat_ref, u_ref, act_ref, *, tiles_per_seq):
    i = pl.program_id(0)
    tm = xm_ref.shape[0]
    first = (i % tiles_per_seq) == 0
    last = (i % tiles_per_seq) == tiles_per_seq - 1
    xcat_ref[0:HALO, :] = jnp.where(first, jnp.zeros_like(xp_ref[...]), xp_ref[...])
    xcat_ref[HALO:HALO + tm, :] = xm_ref[...]
    xcat_ref[HALO + tm:, :] = jnp.where(last, jnp.zeros_like(xn_ref[...]), xn_ref[...])
    xc = xcat_ref[...]

    for j in range(D_FF // FF_TILE):
        slot = j % 2
        halves = []
        for part, base in enumerate((0, D_FF)):
            cols = slice(base + j * FF_TILE, base + (j + 1) * FF_TILE)
            u_ref[slot, part] = jnp.dot(xc, wu_ref[:, cols], preferred_element_type=F32)
            cw = cw_ref[:, cols]
            halves.append(u_ref[slot, part, pl.ds(HALO - 1, tm), :] * cw[0:1, :]
                          + u_ref[slot, part, pl.ds(HALO, tm), :] * cw[1:2, :]
                          + u_ref[slot, part, pl.ds(HALO + 1, tm), :] * cw[2:3, :]
                          + cb_ref[:, cols])
        gate, val = halves
        act_ref[:, j * FF_TILE:(j + 1) * FF_TILE] = (gate * _sigmoid(gate) * val).astype(BF16)

    ffn = jnp.dot(act_ref[...], wd_ref[...], preferred_element_type=F32)
    y = _layer_norm_rows(DEEPNORM_ALPHA * xr_ref[...] + ffn, g_ref[...], b_ref[...])
    o_ref[...] = y
    obf_ref[...] = y.astype(BF16)


def _ffn(xb, x, w_up, conv_w, conv_b, w_down, g, b, tm, seq):
    T, D = x.shape
    hb = tm // HALO
    nhb = T // HALO
    row = lambda i: (i, 0)
    const = lambda i: (0, 0)
    resident = lambda shape: pl.BlockSpec(shape, const, pipeline_mode=pl.Buffered(1))
    return pl.pallas_call(
        functools.partial(_ffn_kernel, tiles_per_seq=seq // tm),
        grid=(T // tm,),
        in_specs=[pl.BlockSpec((HALO, D), lambda i: (jnp.maximum(i * hb - 1, 0), 0)),
                  pl.BlockSpec((tm, D), row),
                  pl.BlockSpec((HALO, D), lambda i: (jnp.minimum((i + 1) * hb, nhb - 1), 0)),
                  pl.BlockSpec((tm, D), row),
                  resident((D, 2 * D_FF)),
                  resident((3, 2 * D_FF)),
                  resident((1, 2 * D_FF)),
                  resident((D_FF, D)),
                  pl.BlockSpec((1, D), const), pl.BlockSpec((1, D), const)],
        out_specs=[pl.BlockSpec((tm, D), row), pl.BlockSpec((tm, D), row)],
        out_shape=[jax.ShapeDtypeStruct((T, D), F32), jax.ShapeDtypeStruct((T, D), BF16)],
        scratch_shapes=[pltpu.VMEM((tm + 2 * HALO, D), BF16),
                        pltpu.VMEM((2, 2, tm + 2 * HALO, FF_TILE), F32),
                        pltpu.VMEM((tm, D_FF), BF16)],
        compiler_params=_cparams(("parallel",)),
        name="conv_ffn_ln",
    )(xb, xb, xb, x, w_up, conv_w, conv_b.reshape(1, -1), w_down, g.reshape(1, D), b.reshape(1, D))


def _rope_tables(positions):
    half = ROPE_DIM // 2
    j = jnp.arange(LANES) % DA_HEAD_DIM
    inv_freq = ROPE_THETA ** (-(2 * (j % half)).astype(F32) / ROPE_DIM)
    inv_freq = jnp.where(j < ROPE_DIM, inv_freq, 0.0)
    sign = jnp.where(j < half, -1.0, 1.0).astype(F32)
    ang = positions.astype(F32).reshape(-1, 1) * inv_freq[None, :]
    return jnp.cos(ang), jnp.sin(ang) * sign


def _pick(n, target):
    t = min(n, target)
    while n % t:
        t //= 2
    return t


def kernel(x, mem, positions, ln_in_g, ln_in_b, w_in, da_lambda, da_norm_g, hg_lb_fwd, hg_lb_bwd,
           hg_norm_g, w_mem_kv, w_out, ln1_g, ln1_b, w_up, conv_w, conv_b, w_down, ln2_g, ln2_b):
    B, S, D = x.shape
    M = mem.shape[1]
    T = B * S
    tm = _pick(S, 1024)
    cosf, sinf = _rope_tables(positions)
    w_in_b, w_kv_b, w_out_b = w_in.astype(BF16), w_mem_kv.astype(BF16), w_out.astype(BF16)
    w_up_b, w_down_b = w_up.astype(BF16), w_down.astype(BF16)
    mem2 = mem.reshape(B * M, D)

    h, hb = _ln_in(x.reshape(T, D), ln_in_g, ln_in_b, tm)
    for l in range(DEPTH):
        pa, pg, pc = _inproj(hb, w_in_b[l], cosf, sinf, hg_lb_fwd, hg_lb_bwd, l, tm)
        pa3 = pa.reshape(B, S, 4 * SEC)
        da_o = _diff_attn(pa3, da_lambda[l], da_norm_g[l], l, _pick(S, 256), _pick(S, 512))
        o_f, o_b = _hgrn(pa3, pg.reshape(B, S, 2 * SEC), pc.reshape(B, S, 3 * SEC))
        mkv = _matmul(mem2, w_kv_b[l], _pick(B * M, 512), 512)
        h, hb = _outproj(da_o.reshape(T, SEC), o_f.reshape(T, SEC), o_b.reshape(T, SEC), pc,
                         hg_norm_g[l], mkv.reshape(B, M, 2 * SEC), w_out_b[l], h, ln1_g[l],
                         ln1_b[l], _pick(S, 512), S)
        h, hb = _ffn(hb, h, w_up_b[l], conv_w[l], conv_b[l], w_down_b[l], ln2_g[l], ln2_b[l],
                     _pick(S, 512), S)
    return h.reshape(B, S, D)
```

```python
import functools
import math

import jax
import jax.numpy as jnp
from jax import lax
from jax.experimental import pallas as pl
from jax.experimental.pallas import tpu as pltpu

F32 = jnp.float32
BF16 = jnp.bfloat16

D_MODEL = 1024
DEPTH = 4
DA_HEADS = 4
DA_HEAD_DIM = 64
HG_HEADS = 4
HG_MIN_FORGET = 1e-20
MX_HEADS = 4
MX_HEAD_DIM = 128
ROPE_THETA = 500000.0
ROPE_DIM = DA_HEAD_DIM // 4
D_FF = 2816
LN_EPS = 1e-5
RMS_EPS = 1e-6
DEEPNORM_ALPHA = (2 * DEPTH) ** 0.25
LOG2E = 1.4426950408889634

LANES = 128
SEC = 512
N_SEC = 9
HG_CHUNK = 128
VMEM_LIMIT = 56 * 1024 * 1024

NT_DIMS = (((1,), (1,)), ((), ()))
TN_DIMS = (((0,), (0,)), ((), ()))


def _cparams(sem):
    return pltpu.CompilerParams(dimension_semantics=sem, vmem_limit_bytes=VMEM_LIMIT)


def _layer_norm_rows(y, g, b):
    mu = jnp.mean(y, axis=-1, keepdims=True)
    d = y - mu
    var = jnp.mean(d * d, axis=-1, keepdims=True)
    return d * lax.rsqrt(var + LN_EPS) * g + b


def _sigmoid(z):
    return 1.0 / (1.0 + jnp.exp(-z))


def _ln_kernel(x_ref, g_ref, b_ref, o_ref, ob_ref):
    y = _layer_norm_rows(x_ref[...], g_ref[...], b_ref[...])
    o_ref[...] = y
    ob_ref[...] = y.astype(BF16)


def _ln_in(x2, g, b, tm):
    T, D = x2.shape
    return pl.pallas_call(
        _ln_kernel,
        grid=(T // tm,),
        in_specs=[pl.BlockSpec((tm, D), lambda i: (i, 0)),
                  pl.BlockSpec((1, D), lambda i: (0, 0)),
                  pl.BlockSpec((1, D), lambda i: (0, 0))],
        out_specs=[pl.BlockSpec((tm, D), lambda i: (i, 0)),
                   pl.BlockSpec((tm, D), lambda i: (i, 0))],
        out_shape=[jax.ShapeDtypeStruct((T, D), F32), jax.ShapeDtypeStruct((T, D), BF16)],
        compiler_params=_cparams(("parallel",)),
        name="ln_in",
    )(x2, g.reshape(1, D), b.reshape(1, D))


def _lower_bound(lb_ref, layer):
    z = lb_ref[...]
    e = jnp.exp(z - jnp.max(z, axis=0, keepdims=True))
    p = e / jnp.sum(e, axis=0, keepdims=True)
    lb = jnp.zeros((1, z.shape[1]), F32)
    for j in range(1, layer + 1):
        lb = lb + p[j:j + 1, :]
    return lb


def _inproj_kernel(x_ref, w_ref, cos_ref, sin_ref, lbf_ref, lbb_ref, a_ref, g_ref, c_ref,
                   *, layer, sub):
    n = pl.program_id(1)
    tm = x_ref.shape[0]

    def run(epilogue, out_ref):
        for r in range(tm // sub):
            rows = slice(r * sub, (r + 1) * sub)
            acc = jnp.dot(x_ref[rows, :], w_ref[...], preferred_element_type=F32)
            out_ref[rows, :] = epilogue(acc, rows).astype(out_ref.dtype)

    def rope(scale):
        def epilogue(acc, rows):
            cs = cos_ref[rows, :]
            sn = sin_ref[rows, :]
            lane = lax.broadcasted_iota(jnp.int32, cs.shape, 1)
            first = (lane & 15) < 8
            parts = []
            for j in range(SEC // LANES):
                t = acc[:, j * LANES:(j + 1) * LANES]
                partner = jnp.where(first, pltpu.roll(t, LANES - 8, axis=1), pltpu.roll(t, 8, axis=1))
                parts.append((t * cs + partner * sn) * scale)
            return jnp.concatenate(parts, axis=1)
        return epilogue

    def log_forget(lb_ref):
        def epilogue(acc, rows):
            lb = _lower_bound(lb_ref, layer)
            f = lb + (1.0 - lb) * _sigmoid(acc)
            return jnp.log(jnp.maximum(f, HG_MIN_FORGET))
        return epilogue

    identity = lambda acc, rows: acc
    silu = lambda acc, rows: acc * _sigmoid(acc)
    branches = [
        (rope(DA_HEAD_DIM ** -0.5 * LOG2E), a_ref),
        (rope(1.0), a_ref),
        (identity, a_ref),
        (silu, a_ref),
        (log_forget(lbf_ref), g_ref),
        (log_forget(lbb_ref), g_ref),
        (identity, c_ref),
        (silu, c_ref),
        (lambda acc, rows: acc * (MX_HEAD_DIM ** -0.5), c_ref),
    ]
    for k, (epilogue, out_ref) in enumerate(branches):
        pl.when(n == k)(functools.partial(run, epilogue, out_ref))


def _inproj(xb, w, cosf, sinf, lbf, lbb, layer, tm):
    T, D = xb.shape
    return pl.pallas_call(
        functools.partial(_inproj_kernel, layer=layer, sub=min(tm, 256)),
        grid=(T // tm, N_SEC),
        in_specs=[pl.BlockSpec((tm, D), lambda i, n: (i, 0)),
                  pl.BlockSpec((D, SEC), lambda i, n: (0, n)),
                  pl.BlockSpec((tm, LANES), lambda i, n: (i, 0)),
                  pl.BlockSpec((tm, LANES), lambda i, n: (i, 0)),
                  pl.BlockSpec((DEPTH, SEC), lambda i, n: (0, 0)),
                  pl.BlockSpec((DEPTH, SEC), lambda i, n: (0, 0))],
        out_specs=[pl.BlockSpec((tm, SEC), lambda i, n: (i, jnp.minimum(n, 3))),
                   pl.BlockSpec((tm, SEC), lambda i, n: (i, jnp.clip(n - 4, 0, 1))),
                   pl.BlockSpec((tm, SEC), lambda i, n: (i, jnp.clip(n - 6, 0, 2)))],
        out_shape=[jax.ShapeDtypeStruct((T, 4 * SEC), BF16),
                   jax.ShapeDtypeStruct((T, 2 * SEC), F32),
                   jax.ShapeDtypeStruct((T, 3 * SEC), BF16)],
        compiler_params=_cparams(("parallel", "arbitrary")),
        name="inproj",
    )(xb, w, cosf, sinf, lbf, lbb)


def _attn_kernel(lam_ref, g_ref, q_ref, k_ref, v_ref, o_ref, vt_ref, sa_ref, sb_ref,
                 *, lam_init, tq, kb):
    S = k_ref.shape[0]
    nq = S // tq
    for j in range(S // kb):
        vt_ref[:, j * kb:(j + 1) * kb] = v_ref[j * kb:(j + 1) * kb, :].astype(F32).T.astype(BF16)

    lf = lam_ref[...]
    lam = (jnp.exp(jnp.sum(lf[0:1, :] * lf[1:2, :], axis=-1, keepdims=True))
           - jnp.exp(jnp.sum(lf[2:3, :] * lf[3:4, :], axis=-1, keepdims=True)) + lam_init)
    lane = lax.broadcasted_iota(jnp.int32, (tq, LANES), 1)

    def step(nxt, cur):
        if nxt is not None:
            q = q_ref[pl.ds(pl.multiple_of(nxt[0] * tq, tq), tq), :]
            kbf = k_ref[...]
        mx, outs = [], []
        for c in range(2):
            if nxt is not None:
                keep = (lane < DA_HEAD_DIM) if c == 0 else (lane >= DA_HEAD_DIM)
                s = lax.dot_general(kbf, jnp.where(keep, q, jnp.zeros_like(q)), NT_DIMS,
                                    preferred_element_type=F32)
                nxt[1][c] = s
                mx.append(jnp.max(s, axis=0, keepdims=True))
            if cur is not None:
                p = jnp.exp2(cur[1][c] - cur[2][c])
                inv = 1.0 / jnp.sum(p, axis=0, keepdims=True)
                outs.append(jnp.dot(vt_ref[...], p.astype(BF16), preferred_element_type=F32) * inv)
        if cur is not None:
            o = (outs[0] - outs[1] * lam).T
            y = o * lax.rsqrt(jnp.mean(o * o, axis=-1, keepdims=True) + RMS_EPS)
            o_ref[pl.ds(pl.multiple_of(cur[0] * tq, tq), tq), :] = (
                y * g_ref[...] * (1.0 - lam_init)).astype(o_ref.dtype)
        return tuple(mx)

    m_a = step((0, sa_ref), None)

    def pair(t, m_a):
        i = 2 * t
        m_b = step((i + 1, sb_ref), (i, sa_ref, m_a))
        return step((i + 2, sa_ref), (i + 1, sb_ref, m_b))

    m_a = lax.fori_loop(0, nq // 2 - 1, pair, m_a)
    m_b = step((nq - 1, sb_ref), (nq - 2, sa_ref, m_a))
    step(None, (nq - 1, sb_ref, m_b))


def _diff_attn(proj3, da_lambda, da_norm_g, layer, tq, kb):
    B, S, _ = proj3.shape
    lam_init = 0.8 - 0.6 * math.exp(-0.3 * layer)
    return pl.pallas_call(
        functools.partial(_attn_kernel, lam_init=lam_init, tq=tq, kb=kb),
        grid=(B, DA_HEADS),
        in_specs=[pl.BlockSpec((4, DA_HEAD_DIM), lambda b, h: (0, 0)),
                  pl.BlockSpec((1, LANES), lambda b, h: (0, 0)),
                  pl.BlockSpec((None, S, LANES), lambda b, h: (b, 0, h)),
                  pl.BlockSpec((None, S, LANES), lambda b, h: (b, 0, 4 + h)),
                  pl.BlockSpec((None, S, LANES), lambda b, h: (b, 0, 8 + h))],
        out_specs=pl.BlockSpec((None, S, LANES), lambda b, h: (b, 0, h)),
        out_shape=jax.ShapeDtypeStruct((B, S, DA_HEADS * LANES), BF16),
        scratch_shapes=[pltpu.VMEM((LANES, S), BF16),
                        pltpu.VMEM((2, S, tq), F32),
                        pltpu.VMEM((2, S, tq), F32)],
        compiler_params=_cparams(("parallel", "parallel")),
        name="diff_attn",
    )(da_lambda, da_norm_g.reshape(1, LANES), proj3, proj3, proj3)


def _bcast_rows(b, idxs, rep):
    return jnp.concatenate(
        [jnp.broadcast_to(b[r:r + 1, :], (rep, b.shape[1])) for r in idxs], axis=0)


def _ref_rows(b, m, reverse, row):
    C = b.shape[0]
    off = m if reverse else m - 1
    if m >= 4:
        return _bcast_rows(b, [G * 2 * m + off for G in range(C // (2 * m))], 2 * m)
    lo = _bcast_rows(b, [8 * G + off for G in range(C // 8)], 8)
    hi = _bcast_rows(b, [8 * G + 4 + off for G in range(C // 8)], 8)
    return jnp.where((row & 7) < 4, lo, hi)


HG_BLOCK = 32
HG_BLOCK_RANGE = 64.0


def _level_masks(C, reverse, block):
    row = lax.broadcasted_iota(jnp.int32, (C, C), 0)
    col = lax.broadcasted_iota(jnp.int32, (C, C), 1)
    masks = {}
    if block is None:
        masks["diag"] = jnp.where(row == col, 1.0, 0.0)
    else:
        shift = block.bit_length() - 1
        causal = (col >= row) if reverse else (col <= row)
        masks["block"] = jnp.where(((row >> shift) == (col >> shift)) & causal, 1.0, 0.0)
    m = 1 if block is None else block
    while m < C:
        shift = m.bit_length() - 1
        q_bit, k_bit = (0, 1) if reverse else (1, 0)
        keep = (((row >> (shift + 1)) == (col >> (shift + 1)))
                & (((row >> shift) & 1) == q_bit) & (((col >> shift) & 1) == k_bit))
        masks[m] = jnp.where(keep, 1.0, 0.0)
        m *= 2
    return masks


def _hgrn_chunk(q, g, v, tri, masks, reverse, block):
    C = q.shape[0]
    row = lax.broadcasted_iota(jnp.int32, (C, LANES), 0)
    f = jnp.exp(g)
    k = (1.0 - f).astype(BF16)

    g1 = g.astype(BF16)
    r1 = g - g1.astype(F32)
    g2 = r1.astype(BF16)
    g3 = (r1 - g2.astype(F32)).astype(BF16)
    cs = jnp.dot(tri, jnp.concatenate([g1, g2, g3], axis=1), preferred_element_type=F32)
    yield None

    b = cs[:, :LANES] + cs[:, LANES:2 * LANES] + cs[:, 2 * LANES:]
    edge = 0 if reverse else C - 1
    b_edge = b[edge:edge + 1, :]
    q_hat = q * jnp.exp2(b * LOG2E).astype(BF16)
    k_hat = k * jnp.exp2((b_edge - b) * LOG2E).astype(BF16)
    upd = lax.dot_general(v, k_hat, TN_DIMS, preferred_element_type=F32)

    scores = []
    if block is None:
        scores.append((lax.dot_general(q, k, NT_DIMS, preferred_element_type=F32), masks["diag"]))
    else:
        off = 0 if reverse else block - 1
        d = (b - _bcast_rows(b, [G * block + off for G in range(C // block)], block)) * LOG2E
        scores.append((lax.dot_general(q * jnp.exp2(d).astype(BF16), k * jnp.exp2(-d).astype(BF16),
                                       NT_DIMS, preferred_element_type=F32), masks["block"]))
    m = 1 if block is None else block
    while m < C:
        if m == 1:
            is_query = (row & 1) == (0 if reverse else 1)
            e = jnp.where(is_query, f, 1.0).astype(BF16)
        else:
            d = b - _ref_rows(b, m, reverse, row)
            e = jnp.exp2(jnp.abs(d) * (-LOG2E)).astype(BF16)
        scores.append((lax.dot_general(q * e, k * e, NT_DIMS, preferred_element_type=F32), masks[m]))
        m *= 2
    state_t = yield None

    attn = scores[0][0] * scores[0][1]
    for s, mask in scores[1:]:
        attn = attn + s * mask
    o = (jnp.dot(attn.astype(BF16), v, preferred_element_type=F32)
         + lax.dot_general(q_hat, state_t.astype(BF16), NT_DIMS, preferred_element_type=F32))
    new_state = state_t * jnp.exp(b_edge) + upd
    yield o, new_state


def _hgrn_kernel(q_ref, gf_ref, gb_ref, v_ref, of_ref, ob_ref):
    S = q_ref.shape[0]
    C = HG_CHUNK
    n = S // C
    row = lax.broadcasted_iota(jnp.int32, (C, C), 0)
    col = lax.broadcasted_iota(jnp.int32, (C, C), 1)
    tri_f = jnp.where(col <= row, 1.0, 0.0).astype(BF16)
    tri_b = jnp.where(col >= row, 1.0, 0.0).astype(BF16)
    per_trip = math.gcd(n, 4)

    def scan(block):
        masks_f = _level_masks(C, False, block)
        masks_b = _level_masks(C, True, block)

        def body(t, carry):
            states = list(carry)
            chains = []
            for u in range(per_trip):
                cf = t * per_trip + u
                for rev, g_ref, tri, masks in ((0, gf_ref, tri_f, masks_f), (1, gb_ref, tri_b, masks_b)):
                    rows = pl.ds(pl.multiple_of((n - 1 - cf if rev else cf) * C, C), C)
                    chains.append((rev, rows, _hgrn_chunk(q_ref[rows, :], g_ref[rows, :], v_ref[rows, :],
                                                          tri, masks, bool(rev), block)))
            for _ in range(2):
                for _, _, gen in chains:
                    next(gen)
            for rev, rows, gen in chains:
                o, states[rev] = gen.send(states[rev])
                (ob_ref if rev else of_ref)[rows, :] = o
            return tuple(states)

        z = jnp.zeros((LANES, LANES), F32)
        lax.fori_loop(0, n // per_trip, body, (z, z))

    def min_block_sum(g_ref):
        tot = jnp.sum(g_ref[...].reshape(S // HG_BLOCK, HG_BLOCK, LANES), axis=1)
        return jnp.min(tot)

    mild = jnp.minimum(min_block_sum(gf_ref), min_block_sum(gb_ref)) >= -HG_BLOCK_RANGE
    pl.when(mild)(functools.partial(scan, HG_BLOCK))
    pl.when(jnp.logical_not(mild))(functools.partial(scan, None))


def _hgrn(pa3, pg3, pc3):
    B, S, _ = pa3.shape
    spec = lambda blk: pl.BlockSpec((None, S, LANES), lambda b, h: (b, 0, blk + h))
    out = jax.ShapeDtypeStruct((B, S, HG_HEADS * LANES), F32)
    return pl.pallas_call(
        _hgrn_kernel,
        grid=(B, HG_HEADS),
        in_specs=[spec(12), spec(0), spec(4), spec(0)],
        out_specs=[pl.BlockSpec((None, S, LANES), lambda b, h: (b, 0, h)),
                   pl.BlockSpec((None, S, LANES), lambda b, h: (b, 0, h))],
        out_shape=[out, out],
        compiler_params=_cparams(("parallel", "parallel")),
        name="hgrn2",
    )(pa3, pg3, pg3, pc3)


def _mm_kernel(x_ref, w_ref, o_ref):
    o_ref[...] = jnp.dot(x_ref[...].astype(BF16), w_ref[...],
                         preferred_element_type=F32).astype(o_ref.dtype)


def _matmul(x, w, tm, tn):
    M, K = x.shape
    N = w.shape[1]
    return pl.pallas_call(
        _mm_kernel,
        grid=(M // tm, N // tn),
        in_specs=[pl.BlockSpec((tm, K), lambda i, j: (i, 0)),
                  pl.BlockSpec((K, tn), lambda i, j: (0, j))],
        out_specs=pl.BlockSpec((tm, tn), lambda i, j: (i, j)),
        out_shape=jax.ShapeDtypeStruct((M, N), BF16),
        compiler_params=_cparams(("parallel", "parallel")),
        name="mem_kv_proj",
    )(x, w)


def _outproj_kernel(da_ref, of_ref, ob_ref, gate_ref, hgn_ref, mq_ref, mkv_ref, w_ref, x_ref,
                    g_ref, b_ref, o_ref, obf_ref):
    hg = of_ref[...] + ob_ref[...]
    gate = gate_ref[...].astype(F32)
    parts = [da_ref[...]]
    for h in range(HG_HEADS):
        sl = slice(h * LANES, (h + 1) * LANES)
        y = hg[:, sl]
        y = y * lax.rsqrt(jnp.mean(y * y, axis=-1, keepdims=True) + RMS_EPS) * hgn_ref[...]
        parts.append((y * gate[:, sl]).astype(BF16))
    width = MX_HEADS * LANES
    for h in range(MX_HEADS):
        sl = slice(h * LANES, (h + 1) * LANES)
        s = lax.dot_general(mq_ref[:, sl], mkv_ref[:, sl], NT_DIMS, preferred_element_type=F32)
        p = jnp.exp(s - jnp.max(s, axis=-1, keepdims=True))
        p = p * (1.0 / jnp.sum(p, axis=-1, keepdims=True))
        mv = mkv_ref[:, width + h * LANES:width + (h + 1) * LANES]
        parts.append(jnp.dot(p.astype(BF16), mv, preferred_element_type=F32).astype(BF16))
    mix = jnp.dot(jnp.concatenate(parts, axis=1), w_ref[...], preferred_element_type=F32)
    y = _layer_norm_rows(DEEPNORM_ALPHA * x_ref[...] + mix, g_ref[...], b_ref[...])
    o_ref[...] = y
    obf_ref[...] = y.astype(BF16)


def _outproj(da_o, o_f, o_b, pc, hg_norm_g, mkv3, w_out, x, g, b, tm, seq):
    T, D = x.shape
    W = SEC
    M = mkv3.shape[1]
    per_seq = seq // tm
    row = lambda i: (i, 0)
    const = lambda i: (0, 0)
    return pl.pallas_call(
        _outproj_kernel,
        grid=(T // tm,),
        in_specs=[pl.BlockSpec((tm, W), row), pl.BlockSpec((tm, W), row), pl.BlockSpec((tm, W), row),
                  pl.BlockSpec((tm, W), lambda i: (i, 1)),
                  pl.BlockSpec((1, LANES), const),
                  pl.BlockSpec((tm, W), lambda i: (i, 2)),
                  pl.BlockSpec((None, M, 2 * W), lambda i: (i // per_seq, 0, 0)),
                  pl.BlockSpec((3 * W, D), const),
                  pl.BlockSpec((tm, D), row),
                  pl.BlockSpec((1, D), const), pl.BlockSpec((1, D), const)],
        out_specs=[pl.BlockSpec((tm, D), row), pl.BlockSpec((tm, D), row)],
        out_shape=[jax.ShapeDtypeStruct((T, D), F32), jax.ShapeDtypeStruct((T, D), BF16)],
        compiler_params=_cparams(("parallel",)),
        name="outproj_ln",
    )(da_o, o_f, o_b, pc, hg_norm_g.reshape(1, LANES), pc, mkv3, w_out, x, g.reshape(1, D),
      b.reshape(1, D))


FF_TILE = 256
HALO = 16


def _ffn_kernel(xp_ref, xm_ref, xn_ref, xr_ref, wu_ref, cw_ref, cb_ref, wd_ref, g_ref, b_ref,
                o_ref, obf_ref, xcat_ref, u_ref, act_ref, *, tiles_per_seq):
    i = pl.program_id(0)
    tm = xm_ref.shape[0]
    first = (i % tiles_per_seq) == 0
    last = (i % tiles_per_seq) == tiles_per_seq - 1
    xcat_ref[0:HALO, :] = jnp.where(first, jnp.zeros_like(xp_ref[...]), xp_ref[...])
    xcat_ref[HALO:HALO + tm, :] = xm_ref[...]
    xcat_ref[HALO + tm:, :] = jnp.where(last, jnp.zeros_like(xn_ref[...]), xn_ref[...])
    xc = xcat_ref[...]

    for j in range(D_FF // FF_TILE):
        slot = j % 2
        halves = []
        for part, base in enumerate((0, D_FF)):
            cols = slice(base + j * FF_TILE, base + (j + 1) * FF_TILE)
            u_ref[slot, part] = jnp.dot(xc, wu_ref[:, cols], preferred_element_type=F32)
            cw = cw_ref[:, cols]
            halves.append(u_ref[slot, part, pl.ds(HALO - 1, tm), :] * cw[0:1, :]
                          + u_ref[slot, part, pl.ds(HALO, tm), :] * cw[1:2, :]
                          + u_ref[slot, part, pl.ds(HALO + 1, tm), :] * cw[2:3, :]
                          + cb_ref[:, cols])
        gate, val = halves
        act_ref[:, j * FF_TILE:(j + 1) * FF_TILE] = (gate * _sigmoid(gate) * val).astype(BF16)

    ffn = jnp.dot(act_ref[...], wd_ref[...], preferred_element_type=F32)
    y = _layer_norm_rows(DEEPNORM_ALPHA * xr_ref[...] + ffn, g_ref[...], b_ref[...])
    o_ref[...] = y
    obf_ref[...] = y.astype(BF16)


def _ffn(xb, x, w_up, conv_w, conv_b, w_down, g, b, tm, seq):
    T, D = x.shape
    hb = tm // HALO
    nhb = T // HALO
    row = lambda i: (i, 0)
    const = lambda i: (0, 0)
    resident = lambda shape: pl.BlockSpec(shape, const, pipeline_mode=pl.Buffered(1))
    return pl.pallas_call(
        functools.partial(_ffn_kernel, tiles_per_seq=seq // tm),
        grid=(T // tm,),
        in_specs=[pl.BlockSpec((HALO, D), lambda i: (jnp.maximum(i * hb - 1, 0), 0)),
                  pl.BlockSpec((tm, D), row),
                  pl.BlockSpec((HALO, D), lambda i: (jnp.minimum((i + 1) * hb, nhb - 1), 0)),
                  pl.BlockSpec((tm, D), row),
                  resident((D, 2 * D_FF)),
                  resident((3, 2 * D_FF)),
                  resident((1, 2 * D_FF)),
                  resident((D_FF, D)),
                  pl.BlockSpec((1, D), const), pl.BlockSpec((1, D), const)],
        out_specs=[pl.BlockSpec((tm, D), row), pl.BlockSpec((tm, D), row)],
        out_shape=[jax.ShapeDtypeStruct((T, D), F32), jax.ShapeDtypeStruct((T, D), BF16)],
        scratch_shapes=[pltpu.VMEM((tm + 2 * HALO, D), BF16),
                        pltpu.VMEM((2, 2, tm + 2 * HALO, FF_TILE), F32),
                        pltpu.VMEM((tm, D_FF), BF16)],
        compiler_params=_cparams(("parallel",)),
        name="conv_ffn_ln",
    )(xb, xb, xb, x, w_up, conv_w, conv_b.reshape(1, -1), w_down, g.reshape(1, D), b.reshape(1, D))


def _rope_tables(positions):
    half = ROPE_DIM // 2
    j = jnp.arange(LANES) % DA_HEAD_DIM
    inv_freq = ROPE_THETA ** (-(2 * (j % half)).astype(F32) / ROPE_DIM)
    inv_freq = jnp.where(j < ROPE_DIM, inv_freq, 0.0)
    sign = jnp.where(j < half, -1.0, 1.0).astype(F32)
    ang = positions.astype(F32).reshape(-1, 1) * inv_freq[None, :]
    return jnp.cos(ang), jnp.sin(ang) * sign


def _pick(n, target):
    t = min(n, target)
    while n % t:
        t //= 2
    return t


def kernel(x, mem, positions, ln_in_g, ln_in_b, w_in, da_lambda, da_norm_g, hg_lb_fwd, hg_lb_bwd,
           hg_norm_g, w_mem_kv, w_out, ln1_g, ln1_b, w_up, conv_w, conv_b, w_down, ln2_g, ln2_b):
    B, S, D = x.shape
    M = mem.shape[1]
    T = B * S
    tm = _pick(S, 1024)
    cosf, sinf = _rope_tables(positions)
    w_in_b, w_kv_b, w_out_b = w_in.astype(BF16), w_mem_kv.astype(BF16), w_out.astype(BF16)
    w_up_b, w_down_b = w_up.astype(BF16), w_down.astype(BF16)
    mem2 = mem.reshape(B * M, D)

    h, hb = _ln_in(x.reshape(T, D), ln_in_g, ln_in_b, tm)
    for l in range(DEPTH):
        pa, pg, pc = _inproj(hb, w_in_b[l], cosf, sinf, hg_lb_fwd, hg_lb_bwd, l, tm)
        pa3 = pa.reshape(B, S, 4 * SEC)
        da_o = _diff_attn(pa3, da_lambda[l], da_norm_g[l], l, _pick(S, 256), _pick(S, 512))
        o_f, o_b = _hgrn(pa3, pg.reshape(B, S, 2 * SEC), pc.reshape(B, S, 3 * SEC))
        mkv = _matmul(mem2, w_kv_b[l], _pick(B * M, 512), 512)
        h, hb = _outproj(da_o.reshape(T, SEC), o_f.reshape(T, SEC), o_b.reshape(T, SEC), pc,
                         hg_norm_g[l], mkv.reshape(B, M, 2 * SEC), w_out_b[l], h, ln1_g[l],
                         ln1_b[l], _pick(S, 512), S)
        h, hb = _ffn(hb, h, w_up_b[l], conv_w[l], conv_b[l], w_down_b[l], ln2_g[l], ln2_b[l],
                     _pick(S, 512), S)
    return h.reshape(B, S, D)
```

```python
import functools
import math

import jax
import jax.numpy as jnp
from jax import lax
from jax.experimental import pallas as pl
from jax.experimental.pallas import tpu as pltpu

F32 = jnp.float32
BF16 = jnp.bfloat16

D_MODEL = 1024
DEPTH = 4
DA_HEADS = 4
DA_HEAD_DIM = 64
HG_HEADS = 4
HG_MIN_FORGET = 1e-20
MX_HEADS = 4
MX_HEAD_DIM = 128
ROPE_THETA = 500000.0
ROPE_DIM = DA_HEAD_DIM // 4
D_FF = 2816
LN_EPS = 1e-5
RMS_EPS = 1e-6
DEEPNORM_ALPHA = (2 * DEPTH) ** 0.25
LOG2E = 1.4426950408889634

LANES = 128
SEC = 512
N_SEC = 9
HG_CHUNK = 128
VMEM_LIMIT = 56 * 1024 * 1024

NT_DIMS = (((1,), (1,)), ((), ()))
TN_DIMS = (((0,), (0,)), ((), ()))


def _cparams(sem):
    return pltpu.CompilerParams(dimension_semantics=sem, vmem_limit_bytes=VMEM_LIMIT)


def _layer_norm_rows(y, g, b):
    mu = jnp.mean(y, axis=-1, keepdims=True)
    d = y - mu
    var = jnp.mean(d * d, axis=-1, keepdims=True)
    return d * lax.rsqrt(var + LN_EPS) * g + b


def _sigmoid(z):
    return 1.0 / (1.0 + jnp.exp(-z))


def _ln_kernel(x_ref, g_ref, b_ref, o_ref, ob_ref):
    y = _layer_norm_rows(x_ref[...], g_ref[...], b_ref[...])
    o_ref[...] = y
    ob_ref[...] = y.astype(BF16)


def _ln_in(x2, g, b, tm):
    T, D = x2.shape
    return pl.pallas_call(
        _ln_kernel,
        grid=(T // tm,),
        in_specs=[pl.BlockSpec((tm, D), lambda i: (i, 0)),
                  pl.BlockSpec((1, D), lambda i: (0, 0)),
                  pl.BlockSpec((1, D), lambda i: (0, 0))],
        out_specs=[pl.BlockSpec((tm, D), lambda i: (i, 0)),
                   pl.BlockSpec((tm, D), lambda i: (i, 0))],
        out_shape=[jax.ShapeDtypeStruct((T, D), F32), jax.ShapeDtypeStruct((T, D), BF16)],
        compiler_params=_cparams(("parallel",)),
        name="ln_in",
    )(x2, g.reshape(1, D), b.reshape(1, D))


def _lower_bound(lb_ref, layer):
    z = lb_ref[...]
    e = jnp.exp(z - jnp.max(z, axis=0, keepdims=True))
    p = e / jnp.sum(e, axis=0, keepdims=True)
    lb = jnp.zeros((1, z.shape[1]), F32)
    for j in range(1, layer + 1):
        lb = lb + p[j:j + 1, :]
    return lb


def _inproj_kernel(x_ref, w_ref, cos_ref, sin_ref, lbf_ref, lbb_ref, a_ref, g_ref, c_ref,
                   *, layer, sub):
    tm = x_ref.shape[0]

    def rope(scale):
        def epilogue(acc, rows):
            cs = cos_ref[rows, :]
            sn = sin_ref[rows, :]
            lane = lax.broadcasted_iota(jnp.int32, cs.shape, 1)
            first = (lane & 15) < 8
            parts = []
            for j in range(SEC // LANES):
                t = acc[:, j * LANES:(j + 1) * LANES]
                partner = jnp.where(first, pltpu.roll(t, LANES - 8, axis=1), pltpu.roll(t, 8, axis=1))
                parts.append((t * cs + partner * sn) * scale)
            return jnp.concatenate(parts, axis=1)
        return epilogue

    def log_forget(lb_ref):
        lb = _lower_bound(lb_ref, layer)

        def epilogue(acc, rows):
            f = lb + (1.0 - lb) * _sigmoid(acc)
            return jnp.log(jnp.maximum(f, HG_MIN_FORGET))
        return epilogue

    identity = lambda acc, rows: acc
    silu = lambda acc, rows: acc * _sigmoid(acc)
    sections = [
        (rope(DA_HEAD_DIM ** -0.5 * LOG2E), a_ref, 0),
        (rope(1.0), a_ref, 1),
        (identity, a_ref, 2),
        (silu, a_ref, 3),
        (log_forget(lbf_ref), g_ref, 0),
        (log_forget(lbb_ref), g_ref, 1),
        (identity, c_ref, 0),
        (silu, c_ref, 1),
        (lambda acc, rows: acc * (MX_HEAD_DIM ** -0.5), c_ref, 2),
    ]
    for n, (epilogue, out_ref, slot) in enumerate(sections):
        for r in range(tm // sub):
            rows = slice(r * sub, (r + 1) * sub)
            acc = jnp.dot(x_ref[rows, :], w_ref[:, n * SEC:(n + 1) * SEC],
                          preferred_element_type=F32)
            out_ref[rows, slot * SEC:(slot + 1) * SEC] = epilogue(acc, rows).astype(out_ref.dtype)


def _inproj(xb, w, cosf, sinf, lbf, lbb, layer, tm):
    T, D = xb.shape
    row = lambda i: (i, 0)
    const = lambda i: (0, 0)
    return pl.pallas_call(
        functools.partial(_inproj_kernel, layer=layer, sub=min(tm, 256)),
        grid=(T // tm,),
        in_specs=[pl.BlockSpec((tm, D), row),
                  pl.BlockSpec((D, N_SEC * SEC), const, pipeline_mode=pl.Buffered(1)),
                  pl.BlockSpec((tm, LANES), row),
                  pl.BlockSpec((tm, LANES), row),
                  pl.BlockSpec((DEPTH, SEC), const),
                  pl.BlockSpec((DEPTH, SEC), const)],
        out_specs=[pl.BlockSpec((tm, 4 * SEC), row),
                   pl.BlockSpec((tm, 2 * SEC), row),
                   pl.BlockSpec((tm, 3 * SEC), row)],
        out_shape=[jax.ShapeDtypeStruct((T, 4 * SEC), BF16),
                   jax.ShapeDtypeStruct((T, 2 * SEC), F32),
                   jax.ShapeDtypeStruct((T, 3 * SEC), BF16)],
        compiler_params=_cparams(("parallel",)),
        name="inproj",
    )(xb, w, cosf, sinf, lbf, lbb)


def _attn_kernel(lam_ref, g_ref, q_ref, k_ref, v_ref, o_ref, vt_ref, sa_ref, sb_ref,
                 *, lam_init, tq, kb):
    S = k_ref.shape[0]
    nq = S // tq
    for j in range(S // kb):
        vt_ref[:, j * kb:(j + 1) * kb] = v_ref[j * kb:(j + 1) * kb, :].astype(F32).T.astype(BF16)

    lf = lam_ref[...]
    lam = (jnp.exp(jnp.sum(lf[0:1, :] * lf[1:2, :], axis=-1, keepdims=True))
           - jnp.exp(jnp.sum(lf[2:3, :] * lf[3:4, :], axis=-1, keepdims=True)) + lam_init)
    lane = lax.broadcasted_iota(jnp.int32, (tq, LANES), 1)

    def step(nxt, cur):
        if nxt is not None:
            q = q_ref[pl.ds(pl.multiple_of(nxt[0] * tq, tq), tq), :]
            kbf = k_ref[...]
        mx, outs = [], []
        for c in range(2):
            if nxt is not None:
                keep = (lane < DA_HEAD_DIM) if c == 0 else (lane >= DA_HEAD_DIM)
                s = lax.dot_general(kbf, jnp.where(keep, q, jnp.zeros_like(q)), NT_DIMS,
                                    preferred_element_type=F32)
                nxt[1][c] = s
                mx.append(jnp.max(s, axis=0, keepdims=True))
            if cur is not None:
                p = jnp.exp2(cur[1][c] - cur[2][c])
                inv = 1.0 / jnp.sum(p, axis=0, keepdims=True)
                outs.append(jnp.dot(vt_ref[...], p.astype(BF16), preferred_element_type=F32) * inv)
        if cur is not None:
            o = (outs[0] - outs[1] * lam).T
            y = o * lax.rsqrt(jnp.mean(o * o, axis=-1, keepdims=True) + RMS_EPS)
            o_ref[pl.ds(pl.multiple_of(cur[0] * tq, tq), tq), :] = (
                y * g_ref[...] * (1.0 - lam_init)).astype(o_ref.dtype)
        return tuple(mx)

    m_a = step((0, sa_ref), None)

    def pair(t, m_a):
        i = 2 * t
        m_b = step((i + 1, sb_ref), (i, sa_ref, m_a))
        return step((i + 2, sa_ref), (i + 1, sb_ref, m_b))

    m_a = lax.fori_loop(0, nq // 2 - 1, pair, m_a)
    m_b = step((nq - 1, sb_ref), (nq - 2, sa_ref, m_a))
    step(None, (nq - 1, sb_ref, m_b))


def _diff_attn(proj3, da_lambda, da_norm_g, layer, tq, kb):
    B, S, _ = proj3.shape
    lam_init = 0.8 - 0.6 * math.exp(-0.3 * layer)
    return pl.pallas_call(
        functools.partial(_attn_kernel, lam_init=lam_init, tq=tq, kb=kb),
        grid=(B, DA_HEADS),
        in_specs=[pl.BlockSpec((4, DA_HEAD_DIM), lambda b, h: (0, 0)),
                  pl.BlockSpec((1, LANES), lambda b, h: (0, 0)),
                  pl.BlockSpec((None, S, LANES), lambda b, h: (b, 0, h)),
                  pl.BlockSpec((None, S, LANES), lambda b, h: (b, 0, 4 + h)),
                  pl.BlockSpec((None, S, LANES), lambda b, h: (b, 0, 8 + h))],
        out_specs=pl.BlockSpec((None, S, LANES), lambda b, h: (b, 0, h)),
        out_shape=jax.ShapeDtypeStruct((B, S, DA_HEADS * LANES), BF16),
        scratch_shapes=[pltpu.VMEM((LANES, S), BF16),
                        pltpu.VMEM((2, S, tq), F32),
                        pltpu.VMEM((2, S, tq), F32)],
        compiler_params=_cparams(("parallel", "parallel")),
        name="diff_attn",
    )(da_lambda, da_norm_g.reshape(1, LANES), proj3, proj3, proj3)


def _bcast_rows(b, idxs, rep):
    return jnp.concatenate(
        [jnp.broadcast_to(b[r:r + 1, :], (rep, b.shape[1])) for r in idxs], axis=0)


def _ref_rows(b, m, reverse, row):
    C = b.shape[0]
    off = m if reverse else m - 1
    if m >= 4:
        return _bcast_rows(b, [G * 2 * m + off for G in range(C // (2 * m))], 2 * m)
    lo = _bcast_rows(b, [8 * G + off for G in range(C // 8)], 8)
    hi = _bcast_rows(b, [8 * G + 4 + off for G in range(C // 8)], 8)
    return jnp.where((row & 7) < 4, lo, hi)


HG_BLOCK = 32
HG_BLOCK_RANGE = 64.0


def _level_masks(C, reverse, block):
    row = lax.broadcasted_iota(jnp.int32, (C, C), 0)
    col = lax.broadcasted_iota(jnp.int32, (C, C), 1)
    masks = {}
    if block is None:
        masks["diag"] = jnp.where(row == col, 1.0, 0.0)
    else:
        shift = block.bit_length() - 1
        causal = (col >= row) if reverse else (col <= row)
        masks["block"] = jnp.where(((row >> shift) == (col >> shift)) & causal, 1.0, 0.0)
    m = 1 if block is None else block
    while m < C:
        shift = m.bit_length() - 1
        q_bit, k_bit = (0, 1) if reverse else (1, 0)
        keep = (((row >> (shift + 1)) == (col >> (shift + 1)))
                & (((row >> shift) & 1) == q_bit) & (((col >> shift) & 1) == k_bit))
        masks[m] = jnp.where(keep, 1.0, 0.0)
        m *= 2
    return masks


def _hgrn_chunk(q, g, v, tri, masks, reverse, block):
    C = q.shape[0]
    row = lax.broadcasted_iota(jnp.int32, (C, LANES), 0)
    f = jnp.exp(g)
    k = (1.0 - f).astype(BF16)

    g1 = g.astype(BF16)
    r1 = g - g1.astype(F32)
    g2 = r1.astype(BF16)
    g3 = (r1 - g2.astype(F32)).astype(BF16)
    cs = jnp.dot(tri, jnp.concatenate([g1, g2, g3], axis=1), preferred_element_type=F32)
    yield None

    b = cs[:, :LANES] + cs[:, LANES:2 * LANES] + cs[:, 2 * LANES:]
    edge = 0 if reverse else C - 1
    b_edge = b[edge:edge + 1, :]
    q_hat = q * jnp.exp2(b * LOG2E).astype(BF16)
    k_hat = k * jnp.exp2((b_edge - b) * LOG2E).astype(BF16)
    upd = lax.dot_general(v, k_hat, TN_DIMS, preferred_element_type=F32)

    scores = []
    if block is None:
        scores.append((lax.dot_general(q, k, NT_DIMS, preferred_element_type=F32), masks["diag"]))
    else:
        off = 0 if reverse else block - 1
        d = (b - _bcast_rows(b, [G * block + off for G in range(C // block)], block)) * LOG2E
        scores.append((lax.dot_general(q * jnp.exp2(d).astype(BF16), k * jnp.exp2(-d).astype(BF16),
                                       NT_DIMS, preferred_element_type=F32), masks["block"]))
    m = 1 if block is None else block
    while m < C:
        if m == 1:
            is_query = (row & 1) == (0 if reverse else 1)
            e = jnp.where(is_query, f, 1.0).astype(BF16)
        else:
            d = b - _ref_rows(b, m, reverse, row)
            e = jnp.exp2(jnp.abs(d) * (-LOG2E)).astype(BF16)
        scores.append((lax.dot_general(q * e, k * e, NT_DIMS, preferred_element_type=F32), masks[m]))
        m *= 2
    state_t = yield None

    attn = scores[0][0] * scores[0][1]
    for s, mask in scores[1:]:
        attn = attn + s * mask
    o = (jnp.dot(attn.astype(BF16), v, preferred_element_type=F32)
         + lax.dot_general(q_hat, state_t.astype(BF16), NT_DIMS, preferred_element_type=F32))
    new_state = state_t * jnp.exp(b_edge) + upd
    yield o, new_state


def _hgrn_kernel(q_ref, gf_ref, gb_ref, v_ref, of_ref, ob_ref):
    S = q_ref.shape[0]
    C = HG_CHUNK
    n = S // C
    row = lax.broadcasted_iota(jnp.int32, (C, C), 0)
    col = lax.broadcasted_iota(jnp.int32, (C, C), 1)
    tri_f = jnp.where(col <= row, 1.0, 0.0).astype(BF16)
    tri_b = jnp.where(col >= row, 1.0, 0.0).astype(BF16)
    per_trip = math.gcd(n, 4)

    def scan(block):
        masks_f = _level_masks(C, False, block)
        masks_b = _level_masks(C, True, block)

        def body(t, carry):
            states = list(carry)
            chains = []
            for u in range(per_trip):
                cf = t * per_trip + u
                for rev, g_ref, tri, masks in ((0, gf_ref, tri_f, masks_f), (1, gb_ref, tri_b, masks_b)):
                    rows = pl.ds(pl.multiple_of((n - 1 - cf if rev else cf) * C, C), C)
                    chains.append((rev, rows, _hgrn_chunk(q_ref[rows, :], g_ref[rows, :], v_ref[rows, :],
                                                          tri, masks, bool(rev), block)))
            for _ in range(2):
                for _, _, gen in chains:
                    next(gen)
            for rev, rows, gen in chains:
                o, states[rev] = gen.send(states[rev])
                (ob_ref if rev else of_ref)[rows, :] = o
            return tuple(states)

        z = jnp.zeros((LANES, LANES), F32)
        lax.fori_loop(0, n // per_trip, body, (z, z))

    def min_block_sum(g_ref):
        tot = jnp.sum(g_ref[...].reshape(S // HG_BLOCK, HG_BLOCK, LANES), axis=1)
        return jnp.min(tot)

    mild = jnp.minimum(min_block_sum(gf_ref), min_block_sum(gb_ref)) >= -HG_BLOCK_RANGE
    pl.when(mild)(functools.partial(scan, HG_BLOCK))
    pl.when(jnp.logical_not(mild))(functools.partial(scan, None))


def _hgrn(pa3, pg3, pc3):
    B, S, _ = pa3.shape
    spec = lambda blk: pl.BlockSpec((None, S, LANES), lambda b, h: (b, 0, blk + h))
    out = jax.ShapeDtypeStruct((B, S, HG_HEADS * LANES), F32)
    return pl.pallas_call(
        _hgrn_kernel,
        grid=(B, HG_HEADS),
        in_specs=[spec(12), spec(0), spec(4), spec(0)],
        out_specs=[pl.BlockSpec((None, S, LANES), lambda b, h: (b, 0, h)),
                   pl.BlockSpec((None, S, LANES), lambda b, h: (b, 0, h))],
        out_shape=[out, out],
        compiler_params=_cparams(("parallel", "parallel")),
        name="hgrn2",
    )(pa3, pg3, pg3, pc3)


def _mm_kernel(x_ref, w_ref, o_ref):
    o_ref[...] = jnp.dot(x_ref[...].astype(BF16), w_ref[...],
                         preferred_element_type=F32).astype(o_ref.dtype)


def _matmul(x, w, tm, tn):
    M, K = x.shape
    N = w.shape[1]
    return pl.pallas_call(
        _mm_kernel,
        grid=(M // tm, N // tn),
        in_specs=[pl.BlockSpec((tm, K), lambda i, j: (i, 0)),
                  pl.BlockSpec((K, tn), lambda i, j: (0, j))],
        out_specs=pl.BlockSpec((tm, tn), lambda i, j: (i, j)),
        out_shape=jax.ShapeDtypeStruct((M, N), BF16),
        compiler_params=_cparams(("parallel", "parallel")),
        name="mem_kv_proj",
    )(x, w)


def _outproj_kernel(da_ref, of_ref, ob_ref, gate_ref, hgn_ref, mq_ref, mkv_ref, w_ref, x_ref,
                    g_ref, b_ref, o_ref, obf_ref):
    hg = of_ref[...] + ob_ref[...]
    gate = gate_ref[...].astype(F32)
    parts = [da_ref[...]]
    for h in range(HG_HEADS):
        sl = slice(h * LANES, (h + 1) * LANES)
        y = hg[:, sl]
        y = y * lax.rsqrt(jnp.mean(y * y, axis=-1, keepdims=True) + RMS_EPS) * hgn_ref[...]
        parts.append((y * gate[:, sl]).astype(BF16))
    width = MX_HEADS * LANES
    for h in range(MX_HEADS):
        sl = slice(h * LANES, (h + 1) * LANES)
        s = lax.dot_general(mq_ref[:, sl], mkv_ref[:, sl], NT_DIMS, preferred_element_type=F32)
        p = jnp.exp(s - jnp.max(s, axis=-1, keepdims=True))
        p = p * (1.0 / jnp.sum(p, axis=-1, keepdims=True))
        mv = mkv_ref[:, width + h * LANES:width + (h + 1) * LANES]
        parts.append(jnp.dot(p.astype(BF16), mv, preferred_element_type=F32).astype(BF16))
    mix = jnp.dot(jnp.concatenate(parts, axis=1), w_ref[...], preferred_element_type=F32)
    y = _layer_norm_rows(DEEPNORM_ALPHA * x_ref[...] + mix, g_ref[...], b_ref[...])
    o_ref[...] = y
    obf_ref[...] = y.astype(BF16)


def _outproj(da_o, o_f, o_b, pc, hg_norm_g, mkv3, w_out, x, g, b, tm, seq):
    T, D = x.shape
    W = SEC
    M = mkv3.shape[1]
    per_seq = seq // tm
    row = lambda i: (i, 0)
    const = lambda i: (0, 0)
    return pl.pallas_call(
        _outproj_kernel,
        grid=(T // tm,),
        in_specs=[pl.BlockSpec((tm, W), row), pl.BlockSpec((tm, W), row), pl.BlockSpec((tm, W), row),
                  pl.BlockSpec((tm, W), lambda i: (i, 1)),
                  pl.BlockSpec((1, LANES), const),
                  pl.BlockSpec((tm, W), lambda i: (i, 2)),
                  pl.BlockSpec((None, M, 2 * W), lambda i: (i // per_seq, 0, 0)),
                  pl.BlockSpec((3 * W, D), const),
                  pl.BlockSpec((tm, D), row),
                  pl.BlockSpec((1, D), const), pl.BlockSpec((1, D), const)],
        out_specs=[pl.BlockSpec((tm, D), row), pl.BlockSpec((tm, D), row)],
        out_shape=[jax.ShapeDtypeStruct((T, D), F32), jax.ShapeDtypeStruct((T, D), BF16)],
        compiler_params=_cparams(("parallel",)),
        name="outproj_ln",
    )(da_o, o_f, o_b, pc, hg_norm_g.reshape(1, LANES), pc, mkv3, w_out, x, g.reshape(1, D),
      b.reshape(1, D))


FF_TILE = 256
HALO = 16


def _ffn_kernel(xp_ref, xm_ref, xn_ref, xr_ref, wu_ref, cw_ref, cb_ref, wd_ref, g_ref, b_ref,
                o_ref, obf_ref, xcat_ref, u_ref, act_ref, *, tiles_per_seq):
    i = pl.program_id(0)
    tm = xm_ref.shape[0]
    first = (i % tiles_per_seq) == 0
    last = (i % tiles_per_seq) == tiles_per_seq - 1
    xcat_ref[0:HALO, :] = jnp.where(first, jnp.zeros_like(xp_ref[...]), xp_ref[...])
    xcat_ref[HALO:HALO + tm, :] = xm_ref[...]
    xcat_ref[HALO + tm:, :] = jnp.where(last, jnp.zeros_like(xn_ref[...]), xn_ref[...])
    xc = xcat_ref[...]

    for j in range(D_FF // FF_TILE):
        slot = j % 2
        halves = []
        for part, base in enumerate((0, D_FF)):
            cols = slice(base + j * FF_TILE, base + (j + 1) * FF_TILE)
            u_ref[slot, part] = jnp.dot(xc, wu_ref[:, cols], preferred_element_type=F32)
            cw = cw_ref[:, cols]
            halves.append(u_ref[slot, part, pl.ds(HALO - 1, tm), :] * cw[0:1, :]
                          + u_ref[slot, part, pl.ds(HALO, tm), :] * cw[1:2, :]
                          + u_ref[slot, part, pl.ds(HALO + 1, tm), :] * cw[2:3, :]
                          + cb_ref[:, cols])
        gate, val = halves
        act_ref[:, j * FF_TILE:(j + 1) * FF_TILE] = (gate * _sigmoid(gate) * val).astype(BF16)

    ffn = jnp.dot(act_ref[...], wd_ref[...], preferred_element_type=F32)
    y = _layer_norm_rows(DEEPNORM_ALPHA * xr_ref[...] + ffn, g_ref[...], b_ref[...])
    o_ref[...] = y
    obf_ref[...] = y.astype(BF16)


def _ffn(xb, x, w_up, conv_w, conv_b, w_down, g, b, tm, seq):
    T, D = x.shape
    hb = tm // HALO
    nhb = T // HALO
    row = lambda i: (i, 0)
    const = lambda i: (0, 0)
    resident = lambda shape: pl.BlockSpec(shape, const, pipeline_mode=pl.Buffered(1))
    return pl.pallas_call(
        functools.partial(_ffn_kernel, tiles_per_seq=seq // tm),
        grid=(T // tm,),
        in_specs=[pl.BlockSpec((HALO, D), lambda i: (jnp.maximum(i * hb - 1, 0), 0)),
                  pl.BlockSpec((tm, D), row),
                  pl.BlockSpec((HALO, D), lambda i: (jnp.minimum((i + 1) * hb, nhb - 1), 0)),
                  pl.BlockSpec((tm, D), row),
                  resident((D, 2 * D_FF)),
                  resident((3, 2 * D_FF)),
                  resident((1, 2 * D_FF)),
                  resident((D_FF, D)),
                  pl.BlockSpec((1, D), const), pl.BlockSpec((1, D), const)],
        out_specs=[pl.BlockSpec((tm, D), row), pl.BlockSpec((tm, D), row)],
        out_shape=[jax.ShapeDtypeStruct((T, D), F32), jax.ShapeDtypeStruct((T, D), BF16)],
        scratch_shapes=[pltpu.VMEM((tm + 2 * HALO, D), BF16),
                        pltpu.VMEM((2, 2, tm + 2 * HALO, FF_TILE), F32),
                        pltpu.VMEM((tm, D_FF), BF16)],
        compiler_params=_cparams(("parallel",)),
        name="conv_ffn_ln",
    )(xb, xb, xb, x, w_up, conv_w, conv_b.reshape(1, -1), w_down, g.reshape(1, D), b.reshape(1, D))


def _rope_tables(positions):
    half = ROPE_DIM // 2
    inv_freq = ROPE_THETA ** (-jnp.arange(0, ROPE_DIM, 2, dtype=F32) / ROPE_DIM)
    ang = positions.astype(F32).reshape(-1, 1) * inv_freq[None, :]
    j = jnp.arange(LANES) % DA_HEAD_DIM
    rotated = (j < ROPE_DIM)[None, :]
    sign = jnp.where(j < half, -1.0, 1.0).astype(F32)[None, :]
    reps = LANES // half
    cosf = jnp.where(rotated, jnp.tile(jnp.cos(ang), (1, reps)), 1.0)
    sinf = jnp.where(rotated, jnp.tile(jnp.sin(ang), (1, reps)) * sign, 0.0)
    return cosf, sinf


def _pick(n, target):
    t = min(n, target)
    while n % t:
        t //= 2
    return t


def kernel(x, mem, positions, ln_in_g, ln_in_b, w_in, da_lambda, da_norm_g, hg_lb_fwd, hg_lb_bwd,
           hg_norm_g, w_mem_kv, w_out, ln1_g, ln1_b, w_up, conv_w, conv_b, w_down, ln2_g, ln2_b):
    B, S, D = x.shape
    M = mem.shape[1]
    T = B * S
    tm = _pick(S, 1024)
    cosf, sinf = _rope_tables(positions)
    mem2 = mem.reshape(B * M, D)
    bf = lambda w, l: w[l].astype(BF16)

    h, hb = _ln_in(x.reshape(T, D), ln_in_g, ln_in_b, tm)
    for l in range(DEPTH):
        pa, pg, pc = _inproj(hb, bf(w_in, l), cosf, sinf, hg_lb_fwd, hg_lb_bwd, l, tm)
        pa3 = pa.reshape(B, S, 4 * SEC)
        da_o = _diff_attn(pa3, da_lambda[l], da_norm_g[l], l, _pick(S, 256), _pick(S, 512))
        o_f, o_b = _hgrn(pa3, pg.reshape(B, S, 2 * SEC), pc.reshape(B, S, 3 * SEC))
        mkv = _matmul(mem2, bf(w_mem_kv, l), _pick(B * M, 512), 512)
        h, hb = _outproj(da_o.reshape(T, SEC), o_f.reshape(T, SEC), o_b.reshape(T, SEC), pc,
                         hg_norm_g[l], mkv.reshape(B, M, 2 * SEC), bf(w_out, l), h, ln1_g[l],
                         ln1_b[l], _pick(S, 512), S)
        h, hb = _ffn(hb, h, bf(w_up, l), conv_w[l], conv_b[l], bf(w_down, l), ln2_g[l], ln2_b[l],
                     _pick(S, 512), S)
    return h.reshape(B, S, D)
```

```python
import functools
import math

import jax
import jax.numpy as jnp
from jax import lax
from jax.experimental import pallas as pl
from jax.experimental.pallas import tpu as pltpu

F32 = jnp.float32
BF16 = jnp.bfloat16

D_MODEL = 1024
DEPTH = 4
DA_HEADS = 4
DA_HEAD_DIM = 64
HG_HEADS = 4
HG_MIN_FORGET = 1e-20
MX_HEADS = 4
MX_HEAD_DIM = 128
ROPE_THETA = 500000.0
ROPE_DIM = DA_HEAD_DIM // 4
D_FF = 2816
LN_EPS = 1e-5
RMS_EPS = 1e-6
DEEPNORM_ALPHA = (2 * DEPTH) ** 0.25
LOG2E = 1.4426950408889634

LANES = 128
SEC = 512
N_SEC = 9
HG_CHUNK = 128
VMEM_LIMIT = 56 * 1024 * 1024

NT_DIMS = (((1,), (1,)), ((), ()))
TN_DIMS = (((0,), (0,)), ((), ()))


def _cparams(sem):
    return pltpu.CompilerParams(dimension_semantics=sem, vmem_limit_bytes=VMEM_LIMIT)


def _layer_norm_rows(y, g, b):
    mu = jnp.mean(y, axis=-1, keepdims=True)
    d = y - mu
    var = jnp.mean(d * d, axis=-1, keepdims=True)
    return d * lax.rsqrt(var + LN_EPS) * g + b


def _sigmoid(z):
    return 1.0 / (1.0 + jnp.exp(-z))


def _ln_kernel(x_ref, g_ref, b_ref, o_ref, ob_ref):
    y = _layer_norm_rows(x_ref[...], g_ref[...], b_ref[...])
    o_ref[...] = y
    ob_ref[...] = y.astype(BF16)


def _ln_in(x2, g, b, tm):
    T, D = x2.shape
    return pl.pallas_call(
        _ln_kernel,
        grid=(T // tm,),
        in_specs=[pl.BlockSpec((tm, D), lambda i: (i, 0)),
                  pl.BlockSpec((1, D), lambda i: (0, 0)),
                  pl.BlockSpec((1, D), lambda i: (0, 0))],
        out_specs=[pl.BlockSpec((tm, D), lambda i: (i, 0)),
                   pl.BlockSpec((tm, D), lambda i: (i, 0))],
        out_shape=[jax.ShapeDtypeStruct((T, D), F32), jax.ShapeDtypeStruct((T, D), BF16)],
        compiler_params=_cparams(("parallel",)),
        name="ln_in",
    )(x2, g.reshape(1, D), b.reshape(1, D))


def _lower_bound(lb_ref, layer):
    z = lb_ref[...]
    e = jnp.exp(z - jnp.max(z, axis=0, keepdims=True))
    p = e / jnp.sum(e, axis=0, keepdims=True)
    lb = jnp.zeros((1, z.shape[1]), F32)
    for j in range(1, layer + 1):
        lb = lb + p[j:j + 1, :]
    return lb


def _inproj_kernel(x_ref, w_ref, cos_ref, sin_ref, lbf_ref, lbb_ref, a_ref, g_ref, c_ref,
                   *, layer, sub):
    tm = x_ref.shape[0]

    def rope(scale):
        def epilogue(acc, rows):
            cs = cos_ref[rows, :]
            sn = sin_ref[rows, :]
            lane = lax.broadcasted_iota(jnp.int32, cs.shape, 1)
            first = (lane & 15) < 8
            parts = []
            for j in range(SEC // LANES):
                t = acc[:, j * LANES:(j + 1) * LANES]
                partner = jnp.where(first, pltpu.roll(t, LANES - 8, axis=1), pltpu.roll(t, 8, axis=1))
                parts.append((t * cs + partner * sn) * scale)
            return jnp.concatenate(parts, axis=1)
        return epilogue

    def log_forget(lb_ref):
        lb = _lower_bound(lb_ref, layer)

        def epilogue(acc, rows):
            f = lb + (1.0 - lb) * _sigmoid(acc)
            return jnp.log(jnp.maximum(f, HG_MIN_FORGET))
        return epilogue

    identity = lambda acc, rows: acc
    silu = lambda acc, rows: acc * _sigmoid(acc)
    sections = [
        (rope(DA_HEAD_DIM ** -0.5 * LOG2E), a_ref, 0),
        (rope(1.0), a_ref, 1),
        (identity, a_ref, 2),
        (silu, a_ref, 3),
        (log_forget(lbf_ref), g_ref, 0),
        (log_forget(lbb_ref), g_ref, 1),
        (identity, c_ref, 0),
        (silu, c_ref, 1),
        (lambda acc, rows: acc * (MX_HEAD_DIM ** -0.5), c_ref, 2),
    ]
    for n, (epilogue, out_ref, slot) in enumerate(sections):
        for r in range(tm // sub):
            rows = slice(r * sub, (r + 1) * sub)
            acc = jnp.dot(x_ref[rows, :], w_ref[:, n * SEC:(n + 1) * SEC],
                          preferred_element_type=F32)
            out_ref[rows, slot * SEC:(slot + 1) * SEC] = epilogue(acc, rows).astype(out_ref.dtype)


def _inproj(xb, w, cosf, sinf, lbf, lbb, layer, tm):
    T, D = xb.shape
    row = lambda i: (i, 0)
    const = lambda i: (0, 0)
    return pl.pallas_call(
        functools.partial(_inproj_kernel, layer=layer, sub=min(tm, 256)),
        grid=(T // tm,),
        in_specs=[pl.BlockSpec((tm, D), row),
                  pl.BlockSpec((None, D, N_SEC * SEC), lambda i: (layer, 0, 0),
                               pipeline_mode=pl.Buffered(1)),
                  pl.BlockSpec((tm, LANES), row),
                  pl.BlockSpec((tm, LANES), row),
                  pl.BlockSpec((DEPTH, SEC), const),
                  pl.BlockSpec((DEPTH, SEC), const)],
        out_specs=[pl.BlockSpec((tm, 4 * SEC), row),
                   pl.BlockSpec((tm, 2 * SEC), row),
                   pl.BlockSpec((tm, 3 * SEC), row)],
        out_shape=[jax.ShapeDtypeStruct((T, 4 * SEC), BF16),
                   jax.ShapeDtypeStruct((T, 2 * SEC), F32),
                   jax.ShapeDtypeStruct((T, 3 * SEC), BF16)],
        compiler_params=_cparams(("parallel",)),
        name="inproj",
    )(xb, w, cosf, sinf, lbf, lbb)


def _attn_kernel(lam_ref, g_ref, q_ref, k_ref, v_ref, o_ref, vt_ref, sa_ref, sb_ref,
                 *, lam_init, tq, kb):
    S = k_ref.shape[0]
    nq = S // tq
    for j in range(S // kb):
        vt_ref[:, j * kb:(j + 1) * kb] = v_ref[j * kb:(j + 1) * kb, :].astype(F32).T.astype(BF16)

    lf = lam_ref[...]
    lam = (jnp.exp(jnp.sum(lf[0:1, :] * lf[1:2, :], axis=-1, keepdims=True))
           - jnp.exp(jnp.sum(lf[2:3, :] * lf[3:4, :], axis=-1, keepdims=True)) + lam_init)
    lane = lax.broadcasted_iota(jnp.int32, (tq, LANES), 1)

    def step(nxt, cur):
        if nxt is not None:
            q = q_ref[pl.ds(pl.multiple_of(nxt[0] * tq, tq), tq), :]
            kbf = k_ref[...]
        mx, outs = [], []
        for c in range(2):
            if nxt is not None:
                keep = (lane < DA_HEAD_DIM) if c == 0 else (lane >= DA_HEAD_DIM)
                s = lax.dot_general(kbf, jnp.where(keep, q, jnp.zeros_like(q)), NT_DIMS,
                                    preferred_element_type=F32)
                nxt[1][c] = s
                mx.append(jnp.max(s, axis=0, keepdims=True))
            if cur is not None:
                p = jnp.exp2(cur[1][c] - cur[2][c])
                inv = 1.0 / jnp.sum(p, axis=0, keepdims=True)
                outs.append(jnp.dot(vt_ref[...], p.astype(BF16), preferred_element_type=F32) * inv)
        if cur is not None:
            o = (outs[0] - outs[1] * lam).T
            y = o * lax.rsqrt(jnp.mean(o * o, axis=-1, keepdims=True) + RMS_EPS)
            o_ref[pl.ds(pl.multiple_of(cur[0] * tq, tq), tq), :] = (
                y * g_ref[...] * (1.0 - lam_init)).astype(o_ref.dtype)
        return tuple(mx)

    bufs = (sa_ref, sb_ref)
    per_trip = 2
    trips = (nq - 1) // per_trip
    mx = step((0, bufs[0]), None)

    def trip(t, mx):
        for u in range(per_trip):
            i = per_trip * t + u
            mx = step((i + 1, bufs[(u + 1) % 2]), (i, bufs[u % 2], mx))
        return mx

    mx = lax.fori_loop(0, trips, trip, mx)
    for i in range(per_trip * trips, nq - 1):
        mx = step((i + 1, bufs[(i + 1) % 2]), (i, bufs[i % 2], mx))
    step(None, (nq - 1, bufs[(nq - 1) % 2], mx))


def _diff_attn(proj3, da_lambda, da_norm_g, layer, tq, kb):
    B, S, _ = proj3.shape
    lam_init = 0.8 - 0.6 * math.exp(-0.3 * layer)
    return pl.pallas_call(
        functools.partial(_attn_kernel, lam_init=lam_init, tq=tq, kb=kb),
        grid=(B, DA_HEADS),
        in_specs=[pl.BlockSpec((4, DA_HEAD_DIM), lambda b, h: (0, 0)),
                  pl.BlockSpec((1, LANES), lambda b, h: (0, 0)),
                  pl.BlockSpec((None, S, LANES), lambda b, h: (b, 0, h)),
                  pl.BlockSpec((None, S, LANES), lambda b, h: (b, 0, 4 + h)),
                  pl.BlockSpec((None, S, LANES), lambda b, h: (b, 0, 8 + h))],
        out_specs=pl.BlockSpec((None, S, LANES), lambda b, h: (b, 0, h)),
        out_shape=jax.ShapeDtypeStruct((B, S, DA_HEADS * LANES), BF16),
        scratch_shapes=[pltpu.VMEM((LANES, S), BF16),
                        pltpu.VMEM((2, S, tq), F32),
                        pltpu.VMEM((2, S, tq), F32)],
        compiler_params=_cparams(("parallel", "parallel")),
        name="diff_attn",
    )(da_lambda, da_norm_g.reshape(1, LANES), proj3, proj3, proj3)


def _bcast_rows(b, idxs, rep):
    return jnp.concatenate(
        [jnp.broadcast_to(b[r:r + 1, :], (rep, b.shape[1])) for r in idxs], axis=0)


def _ref_rows(b, m, reverse, row):
    C = b.shape[0]
    off = m if reverse else m - 1
    if m >= 4:
        return _bcast_rows(b, [G * 2 * m + off for G in range(C // (2 * m))], 2 * m)
    lo = _bcast_rows(b, [8 * G + off for G in range(C // 8)], 8)
    hi = _bcast_rows(b, [8 * G + 4 + off for G in range(C // 8)], 8)
    return jnp.where((row & 7) < 4, lo, hi)


HG_BLOCK = 32
HG_BLOCK_RANGE = 64.0


def _level_masks(C, reverse, block):
    row = lax.broadcasted_iota(jnp.int32, (C, C), 0)
    col = lax.broadcasted_iota(jnp.int32, (C, C), 1)
    masks = {}
    if block is None:
        masks["diag"] = jnp.where(row == col, 1.0, 0.0)
    else:
        shift = block.bit_length() - 1
        causal = (col >= row) if reverse else (col <= row)
        masks["block"] = jnp.where(((row >> shift) == (col >> shift)) & causal, 1.0, 0.0)
    m = 1 if block is None else block
    while m < C:
        shift = m.bit_length() - 1
        q_bit, k_bit = (0, 1) if reverse else (1, 0)
        keep = (((row >> (shift + 1)) == (col >> (shift + 1)))
                & (((row >> shift) & 1) == q_bit) & (((col >> shift) & 1) == k_bit))
        masks[m] = jnp.where(keep, 1.0, 0.0)
        m *= 2
    return masks


def _hgrn_chunk(q, g, v, tri, masks, reverse, block):
    C = q.shape[0]
    row = lax.broadcasted_iota(jnp.int32, (C, LANES), 0)
    f = jnp.exp(g)
    k = (1.0 - f).astype(BF16)

    g1 = g.astype(BF16)
    r1 = g - g1.astype(F32)
    g2 = r1.astype(BF16)
    g3 = (r1 - g2.astype(F32)).astype(BF16)
    cs = jnp.dot(tri, jnp.concatenate([g1, g2, g3], axis=1), preferred_element_type=F32)
    yield None

    b = cs[:, :LANES] + cs[:, LANES:2 * LANES] + cs[:, 2 * LANES:]
    edge = 0 if reverse else C - 1
    b_edge = b[edge:edge + 1, :]
    q_hat = q * jnp.exp2(b * LOG2E).astype(BF16)
    k_hat = k * jnp.exp2((b_edge - b) * LOG2E).astype(BF16)
    upd = lax.dot_general(v, k_hat, TN_DIMS, preferred_element_type=F32)

    scores = []
    if block is None:
        scores.append((lax.dot_general(q, k, NT_DIMS, preferred_element_type=F32), masks["diag"]))
    else:
        off = 0 if reverse else block - 1
        d = (b - _bcast_rows(b, [G * block + off for G in range(C // block)], block)) * LOG2E
        scores.append((lax.dot_general(q * jnp.exp2(d).astype(BF16), k * jnp.exp2(-d).astype(BF16),
                                       NT_DIMS, preferred_element_type=F32), masks["block"]))
    m = 1 if block is None else block
    while m < C:
        if m == 1:
            is_query = (row & 1) == (0 if reverse else 1)
            e = jnp.where(is_query, f, 1.0).astype(BF16)
        else:
            d = b - _ref_rows(b, m, reverse, row)
            e = jnp.exp2(jnp.abs(d) * (-LOG2E)).astype(BF16)
        scores.append((lax.dot_general(q * e, k * e, NT_DIMS, preferred_element_type=F32), masks[m]))
        m *= 2
    state_t = yield None

    attn = scores[0][0] * scores[0][1]
    for s, mask in scores[1:]:
        attn = attn + s * mask
    o = (jnp.dot(attn.astype(BF16), v, preferred_element_type=F32)
         + lax.dot_general(q_hat, state_t.astype(BF16), NT_DIMS, preferred_element_type=F32))
    new_state = state_t * jnp.exp(b_edge) + upd
    yield o, new_state


def _hgrn_kernel(q_ref, gf_ref, gb_ref, v_ref, of_ref, ob_ref):
    S = q_ref.shape[0]
    C = HG_CHUNK
    n = S // C
    row = lax.broadcasted_iota(jnp.int32, (C, C), 0)
    col = lax.broadcasted_iota(jnp.int32, (C, C), 1)
    tri_f = jnp.where(col <= row, 1.0, 0.0).astype(BF16)
    tri_b = jnp.where(col >= row, 1.0, 0.0).astype(BF16)
    per_trip = math.gcd(n, 4)

    def scan(block):
        masks_f = _level_masks(C, False, block)
        masks_b = _level_masks(C, True, block)

        def body(t, carry):
            states = list(carry)
            chains = []
            for u in range(per_trip):
                cf = t * per_trip + u
                for rev, g_ref, tri, masks in ((0, gf_ref, tri_f, masks_f), (1, gb_ref, tri_b, masks_b)):
                    rows = pl.ds(pl.multiple_of((n - 1 - cf if rev else cf) * C, C), C)
                    chains.append((rev, rows, _hgrn_chunk(q_ref[rows, :], g_ref[rows, :], v_ref[rows, :],
                                                          tri, masks, bool(rev), block)))
            for _ in range(2):
                for _, _, gen in chains:
                    next(gen)
            for rev, rows, gen in chains:
                o, states[rev] = gen.send(states[rev])
                (ob_ref if rev else of_ref)[rows, :] = o
            return tuple(states)

        z = jnp.zeros((LANES, LANES), F32)
        lax.fori_loop(0, n // per_trip, body, (z, z))

    def min_block_sum(g_ref):
        tot = jnp.sum(g_ref[...].reshape(S // HG_BLOCK, HG_BLOCK, LANES), axis=1)
        return jnp.min(tot)

    mild = jnp.minimum(min_block_sum(gf_ref), min_block_sum(gb_ref)) >= -HG_BLOCK_RANGE
    pl.when(mild)(functools.partial(scan, HG_BLOCK))
    pl.when(jnp.logical_not(mild))(functools.partial(scan, None))


def _hgrn(pa3, pg3, pc3):
    B, S, _ = pa3.shape
    spec = lambda blk: pl.BlockSpec((None, S, LANES), lambda b, h: (b, 0, blk + h))
    out = jax.ShapeDtypeStruct((B, S, HG_HEADS * LANES), F32)
    return pl.pallas_call(
        _hgrn_kernel,
        grid=(B, HG_HEADS),
        in_specs=[spec(12), spec(0), spec(4), spec(0)],
        out_specs=[pl.BlockSpec((None, S, LANES), lambda b, h: (b, 0, h)),
                   pl.BlockSpec((None, S, LANES), lambda b, h: (b, 0, h))],
        out_shape=[out, out],
        compiler_params=_cparams(("parallel", "parallel")),
        name="hgrn2",
    )(pa3, pg3, pg3, pc3)


def _mm_kernel(x_ref, w_ref, o_ref):
    o_ref[...] = jnp.dot(x_ref[...].astype(BF16), w_ref[...],
                         preferred_element_type=F32).astype(o_ref.dtype)


def _matmul(x, w, layer, tm, tn):
    M, K = x.shape
    N = w.shape[2]
    return pl.pallas_call(
        _mm_kernel,
        grid=(M // tm, N // tn),
        in_specs=[pl.BlockSpec((tm, K), lambda i, j: (i, 0)),
                  pl.BlockSpec((None, K, tn), lambda i, j: (layer, 0, j))],
        out_specs=pl.BlockSpec((tm, tn), lambda i, j: (i, j)),
        out_shape=jax.ShapeDtypeStruct((M, N), BF16),
        compiler_params=_cparams(("parallel", "parallel")),
        name="mem_kv_proj",
    )(x, w)


def _outproj_kernel(da_ref, of_ref, ob_ref, gate_ref, hgn_ref, mq_ref, mkv_ref, w_ref, x_ref,
                    g_ref, b_ref, o_ref, obf_ref):
    hg = of_ref[...] + ob_ref[...]
    gate = gate_ref[...].astype(F32)
    parts = [da_ref[...]]
    for h in range(HG_HEADS):
        sl = slice(h * LANES, (h + 1) * LANES)
        y = hg[:, sl]
        y = y * lax.rsqrt(jnp.mean(y * y, axis=-1, keepdims=True) + RMS_EPS) * hgn_ref[...]
        parts.append((y * gate[:, sl]).astype(BF16))
    width = MX_HEADS * LANES
    for h in range(MX_HEADS):
        sl = slice(h * LANES, (h + 1) * LANES)
        s = lax.dot_general(mq_ref[:, sl], mkv_ref[:, sl], NT_DIMS, preferred_element_type=F32)
        p = jnp.exp(s - jnp.max(s, axis=-1, keepdims=True))
        p = p * (1.0 / jnp.sum(p, axis=-1, keepdims=True))
        mv = mkv_ref[:, width + h * LANES:width + (h + 1) * LANES]
        parts.append(jnp.dot(p.astype(BF16), mv, preferred_element_type=F32).astype(BF16))
    mix = jnp.dot(jnp.concatenate(parts, axis=1), w_ref[...], preferred_element_type=F32)
    y = _layer_norm_rows(DEEPNORM_ALPHA * x_ref[...] + mix, g_ref[...], b_ref[...])
    o_ref[...] = y
    obf_ref[...] = y.astype(BF16)


def _outproj(da_o, o_f, o_b, pc, hg_norm_g, mkv3, w_out, layer, x, g, b, tm, seq):
    T, D = x.shape
    W = SEC
    M = mkv3.shape[1]
    per_seq = seq // tm
    row = lambda i: (i, 0)
    const = lambda i: (0, 0)
    return pl.pallas_call(
        _outproj_kernel,
        grid=(T // tm,),
        in_specs=[pl.BlockSpec((tm, W), row), pl.BlockSpec((tm, W), row), pl.BlockSpec((tm, W), row),
                  pl.BlockSpec((tm, W), lambda i: (i, 1)),
                  pl.BlockSpec((1, LANES), const),
                  pl.BlockSpec((tm, W), lambda i: (i, 2)),
                  pl.BlockSpec((None, M, 2 * W), lambda i: (i // per_seq, 0, 0)),
                  pl.BlockSpec((None, 3 * W, D), lambda i: (layer, 0, 0)),
                  pl.BlockSpec((tm, D), row),
                  pl.BlockSpec((1, D), const), pl.BlockSpec((1, D), const)],
        out_specs=[pl.BlockSpec((tm, D), row), pl.BlockSpec((tm, D), row)],
        out_shape=[jax.ShapeDtypeStruct((T, D), F32), jax.ShapeDtypeStruct((T, D), BF16)],
        compiler_params=_cparams(("parallel",)),
        name="outproj_ln",
    )(da_o, o_f, o_b, pc, hg_norm_g.reshape(1, LANES), pc, mkv3, w_out, x, g.reshape(1, D),
      b.reshape(1, D))


FF_TILE = 256
HALO = 16


def _ffn_kernel(xp_ref, xm_ref, xn_ref, xr_ref, wu_ref, cw_ref, cb_ref, wd_ref, g_ref, b_ref,
                o_ref, obf_ref, xcat_ref, act_ref, *, tiles_per_seq):
    i = pl.program_id(0)
    tm = xm_ref.shape[0]
    first = (i % tiles_per_seq) == 0
    last = (i % tiles_per_seq) == tiles_per_seq - 1
    xcat_ref[0:HALO, :] = jnp.where(first, jnp.zeros_like(xp_ref[...]), xp_ref[...])
    xcat_ref[HALO:HALO + tm, :] = xm_ref[...]
    xcat_ref[HALO + tm:, :] = jnp.where(last, jnp.zeros_like(xn_ref[...]), xn_ref[...])
    xc = xcat_ref[...]

    for j in range(D_FF // FF_TILE):
        halves = []
        for part, base in enumerate((0, D_FF)):
            cols = slice(base + j * FF_TILE, base + (j + 1) * FF_TILE)
            u = jnp.dot(xc, wu_ref[:, cols], preferred_element_type=F32)
            rows = u.shape[0]
            cw = cw_ref[:, cols]
            halves.append(pltpu.roll(u, 1, axis=0)[HALO:HALO + tm, :] * cw[0:1, :]
                          + u[HALO:HALO + tm, :] * cw[1:2, :]
                          + pltpu.roll(u, rows - 1, axis=0)[HALO:HALO + tm, :] * cw[2:3, :]
                          + cb_ref[:, cols])
        gate, val = halves
        act_ref[:, j * FF_TILE:(j + 1) * FF_TILE] = (gate * _sigmoid(gate) * val).astype(BF16)

    ffn = jnp.dot(act_ref[...], wd_ref[...], preferred_element_type=F32)
    y = _layer_norm_rows(DEEPNORM_ALPHA * xr_ref[...] + ffn, g_ref[...], b_ref[...])
    o_ref[...] = y
    obf_ref[...] = y.astype(BF16)


def _ffn(xb, x, w_up, conv_w, conv_b, w_down, layer, g, b, tm, seq):
    T, D = x.shape
    hb = tm // HALO
    nhb = T // HALO
    row = lambda i: (i, 0)
    const = lambda i: (0, 0)
    resident = lambda shape: pl.BlockSpec(shape, const, pipeline_mode=pl.Buffered(1))
    stacked = lambda shape: pl.BlockSpec((None,) + shape, lambda i: (layer, 0, 0),
                                         pipeline_mode=pl.Buffered(1))
    return pl.pallas_call(
        functools.partial(_ffn_kernel, tiles_per_seq=seq // tm),
        grid=(T // tm,),
        in_specs=[pl.BlockSpec((HALO, D), lambda i: (jnp.maximum(i * hb - 1, 0), 0)),
                  pl.BlockSpec((tm, D), row),
                  pl.BlockSpec((HALO, D), lambda i: (jnp.minimum((i + 1) * hb, nhb - 1), 0)),
                  pl.BlockSpec((tm, D), row),
                  stacked((D, 2 * D_FF)),
                  resident((3, 2 * D_FF)),
                  resident((1, 2 * D_FF)),
                  stacked((D_FF, D)),
                  pl.BlockSpec((1, D), const), pl.BlockSpec((1, D), const)],
        out_specs=[pl.BlockSpec((tm, D), row), pl.BlockSpec((tm, D), row)],
        out_shape=[jax.ShapeDtypeStruct((T, D), F32), jax.ShapeDtypeStruct((T, D), BF16)],
        scratch_shapes=[pltpu.VMEM((tm + 2 * HALO, D), BF16),
                        pltpu.VMEM((tm, D_FF), BF16)],
        compiler_params=_cparams(("parallel",)),
        name="conv_ffn_ln",
    )(xb, xb, xb, x, w_up, conv_w, conv_b.reshape(1, -1), w_down, g.reshape(1, D), b.reshape(1, D))


def _rope_tables(positions):
    half = ROPE_DIM // 2
    inv_freq = ROPE_THETA ** (-jnp.arange(0, ROPE_DIM, 2, dtype=F32) / ROPE_DIM)
    ang = positions.astype(F32).reshape(-1, 1) * inv_freq[None, :]
    j = jnp.arange(LANES) % DA_HEAD_DIM
    rotated = (j < ROPE_DIM)[None, :]
    sign = jnp.where(j < half, -1.0, 1.0).astype(F32)[None, :]
    reps = LANES // half
    cosf = jnp.where(rotated, jnp.tile(jnp.cos(ang), (1, reps)), 1.0)
    sinf = jnp.where(rotated, jnp.tile(jnp.sin(ang), (1, reps)) * sign, 0.0)
    return cosf, sinf


def _pick(n, target):
    t = min(n, target)
    while n % t:
        t //= 2
    return t


def kernel(x, mem, positions, ln_in_g, ln_in_b, w_in, da_lambda, da_norm_g, hg_lb_fwd, hg_lb_bwd,
           hg_norm_g, w_mem_kv, w_out, ln1_g, ln1_b, w_up, conv_w, conv_b, w_down, ln2_g, ln2_b):
    B, S, D = x.shape
    M = mem.shape[1]
    T = B * S
    tm = _pick(S, 1024)
    cosf, sinf = _rope_tables(positions)
    mem2 = mem.reshape(B * M, D)
    w_in_b, w_kv_b, w_out_b = w_in.astype(BF16), w_mem_kv.astype(BF16), w_out.astype(BF16)
    w_up_b, w_down_b = w_up.astype(BF16), w_down.astype(BF16)

    h, hb = _ln_in(x.reshape(T, D), ln_in_g, ln_in_b, tm)
    for l in range(DEPTH):
        pa, pg, pc = _inproj(hb, w_in_b, cosf, sinf, hg_lb_fwd, hg_lb_bwd, l, tm)
        pa3 = pa.reshape(B, S, 4 * SEC)
        da_o = _diff_attn(pa3, da_lambda[l], da_norm_g[l], l, _pick(S, 256), _pick(S, 512))
        o_f, o_b = _hgrn(pa3, pg.reshape(B, S, 2 * SEC), pc.reshape(B, S, 3 * SEC))
        mkv = _matmul(mem2, w_kv_b, l, _pick(B * M, 512), 512)
        h, hb = _outproj(da_o.reshape(T, SEC), o_f.reshape(T, SEC), o_b.reshape(T, SEC), pc,
                         hg_norm_g[l], mkv.reshape(B, M, 2 * SEC), w_out_b, l, h, ln1_g[l],
                         ln1_b[l], _pick(S, 512), S)
        h, hb = _ffn(hb, h, w_up_b, conv_w[l], conv_b[l], w_down_b, l, ln2_g[l], ln2_b[l],
                     _pick(S, 512), S)
    return h.reshape(B, S, D)
```

```python
import functools
import math

import jax
import jax.numpy as jnp
from jax import lax
from jax.experimental import pallas as pl
from jax.experimental.pallas import tpu as pltpu

F32 = jnp.float32
BF16 = jnp.bfloat16

D_MODEL = 1024
DEPTH = 4
DA_HEADS = 4
DA_HEAD_DIM = 64
HG_HEADS = 4
HG_MIN_FORGET = 1e-20
MX_HEADS = 4
MX_HEAD_DIM = 128
ROPE_THETA = 500000.0
ROPE_DIM = DA_HEAD_DIM // 4
D_FF = 2816
LN_EPS = 1e-5
RMS_EPS = 1e-6
DEEPNORM_ALPHA = (2 * DEPTH) ** 0.25
LOG2E = 1.4426950408889634

LANES = 128
SEC = 512
N_SEC = 9
HG_CHUNK = 128
VMEM_LIMIT = 56 * 1024 * 1024

NT_DIMS = (((1,), (1,)), ((), ()))
TN_DIMS = (((0,), (0,)), ((), ()))


def _cparams(sem):
    return pltpu.CompilerParams(dimension_semantics=sem, vmem_limit_bytes=VMEM_LIMIT)


def _layer_norm_rows(y, g, b):
    mu = jnp.mean(y, axis=-1, keepdims=True)
    d = y - mu
    var = jnp.mean(d * d, axis=-1, keepdims=True)
    return d * lax.rsqrt(var + LN_EPS) * g + b


def _sigmoid(z):
    return 1.0 / (1.0 + jnp.exp(-z))


def _lower_bound(lb_ref, layer):
    z = lb_ref[...]
    e = jnp.exp(z - jnp.max(z, axis=0, keepdims=True))
    p = e / jnp.sum(e, axis=0, keepdims=True)
    lb = jnp.zeros((1, z.shape[1]), F32)
    for j in range(1, layer + 1):
        lb = lb + p[j:j + 1, :]
    return lb


def _inproj_kernel(*refs, layer, sub, norm_input):
    if norm_input:
        (xin_ref, lng_ref, lnb_ref, w_ref, cos_ref, sin_ref, lbf_ref, lbb_ref,
         a_ref, g_ref, c_ref, h_ref, x_ref) = refs
        h = _layer_norm_rows(xin_ref[...], lng_ref[...], lnb_ref[...])
        h_ref[...] = h
        x_ref[...] = h.astype(BF16)
    else:
        x_ref, w_ref, cos_ref, sin_ref, lbf_ref, lbb_ref, a_ref, g_ref, c_ref = refs
    tm = x_ref.shape[0]

    def rope(scale):
        def epilogue(acc, rows):
            cs = cos_ref[rows, :]
            sn = sin_ref[rows, :]
            lane = lax.broadcasted_iota(jnp.int32, cs.shape, 1)
            first = (lane & 15) < 8
            parts = []
            for j in range(SEC // LANES):
                t = acc[:, j * LANES:(j + 1) * LANES]
                partner = jnp.where(first, pltpu.roll(t, LANES - 8, axis=1), pltpu.roll(t, 8, axis=1))
                parts.append((t * cs + partner * sn) * scale)
            return jnp.concatenate(parts, axis=1)
        return epilogue

    def log_forget(lb_ref):
        lb = _lower_bound(lb_ref, layer)

        def epilogue(acc, rows):
            f = lb + (1.0 - lb) * _sigmoid(acc)
            return jnp.log(jnp.maximum(f, HG_MIN_FORGET))
        return epilogue

    identity = lambda acc, rows: acc
    silu = lambda acc, rows: acc * _sigmoid(acc)
    sections = [
        (rope(DA_HEAD_DIM ** -0.5 * LOG2E), a_ref, 0),
        (rope(1.0), a_ref, 1),
        (identity, a_ref, 2),
        (silu, a_ref, 3),
        (log_forget(lbf_ref), g_ref, 0),
        (log_forget(lbb_ref), g_ref, 1),
        (identity, c_ref, 0),
        (silu, c_ref, 1),
        (lambda acc, rows: acc * (MX_HEAD_DIM ** -0.5), c_ref, 2),
    ]
    for n, (epilogue, out_ref, slot) in enumerate(sections):
        for r in range(tm // sub):
            rows = slice(r * sub, (r + 1) * sub)
            acc = jnp.dot(x_ref[rows, :], w_ref[:, n * SEC:(n + 1) * SEC],
                          preferred_element_type=F32)
            out_ref[rows, slot * SEC:(slot + 1) * SEC] = epilogue(acc, rows).astype(out_ref.dtype)


def _inproj(x, w, cosf, sinf, lbf, lbb, layer, tm, ln=None):
    T, D = x.shape
    row = lambda i: (i, 0)
    const = lambda i: (0, 0)
    in_specs = [pl.BlockSpec((tm, D), row)]
    args = [x]
    out_specs = [pl.BlockSpec((tm, 4 * SEC), row),
                 pl.BlockSpec((tm, 2 * SEC), row),
                 pl.BlockSpec((tm, 3 * SEC), row)]
    out_shape = [jax.ShapeDtypeStruct((T, 4 * SEC), BF16),
                 jax.ShapeDtypeStruct((T, 2 * SEC), F32),
                 jax.ShapeDtypeStruct((T, 3 * SEC), BF16)]
    scratch = []
    if ln is not None:
        in_specs += [pl.BlockSpec((1, D), const), pl.BlockSpec((1, D), const)]
        args += [ln[0].reshape(1, D), ln[1].reshape(1, D)]
        out_specs.append(pl.BlockSpec((tm, D), row))
        out_shape.append(jax.ShapeDtypeStruct((T, D), F32))
        scratch.append(pltpu.VMEM((tm, D), BF16))
    in_specs += [pl.BlockSpec((None, D, N_SEC * SEC), lambda i: (layer, 0, 0),
                              pipeline_mode=pl.Buffered(1)),
                 pl.BlockSpec((tm, LANES), row),
                 pl.BlockSpec((tm, LANES), row),
                 pl.BlockSpec((DEPTH, SEC), const),
                 pl.BlockSpec((DEPTH, SEC), const)]
    args += [w, cosf, sinf, lbf, lbb]
    return pl.pallas_call(
        functools.partial(_inproj_kernel, layer=layer, sub=min(tm, 256),
                          norm_input=ln is not None),
        grid=(T // tm,),
        in_specs=in_specs,
        out_specs=out_specs,
        out_shape=out_shape,
        scratch_shapes=scratch,
        compiler_params=_cparams(("parallel",)),
        name="inproj",
    )(*args)


ATTN_HEADS_PER_STEP = 2


def _attn_kernel(lam_ref, g_ref, *refs, lam_init, tq, kb):
    hps = ATTN_HEADS_PER_STEP
    q_refs, k_refs, v_refs = refs[:hps], refs[hps:2 * hps], refs[2 * hps:3 * hps]
    o_ref, vt_ref, sa_ref, sb_ref = refs[3 * hps:]
    S = k_refs[0].shape[0]
    nq = S // tq
    for hh in range(hps):
        for j in range(S // kb):
            vt_ref[hh, :, j * kb:(j + 1) * kb] = (
                v_refs[hh][j * kb:(j + 1) * kb, :].astype(F32).T.astype(BF16))

    lf = lam_ref[...]
    lam = (jnp.exp(jnp.sum(lf[0:1, :] * lf[1:2, :], axis=-1, keepdims=True))
           - jnp.exp(jnp.sum(lf[2:3, :] * lf[3:4, :], axis=-1, keepdims=True)) + lam_init)
    lane = lax.broadcasted_iota(jnp.int32, (tq, LANES), 1)

    def step(nxt, cur):
        if nxt is not None:
            q = q_refs[nxt[0]][pl.ds(pl.multiple_of(nxt[1] * tq, tq), tq), :]
            kbf = k_refs[nxt[0]][...]
        mx, outs = [], []
        for c in range(2):
            if nxt is not None:
                keep = (lane < DA_HEAD_DIM) if c == 0 else (lane >= DA_HEAD_DIM)
                s = lax.dot_general(kbf, jnp.where(keep, q, jnp.zeros_like(q)), NT_DIMS,
                                    preferred_element_type=F32)
                nxt[2][c] = s
                mx.append(jnp.max(s, axis=0, keepdims=True))
            if cur is not None:
                p = jnp.exp2(cur[2][c] - cur[3][c])
                inv = 1.0 / jnp.sum(p, axis=0, keepdims=True)
                outs.append(jnp.dot(vt_ref[cur[0]], p.astype(BF16), preferred_element_type=F32) * inv)
        if cur is not None:
            o = (outs[0] - outs[1] * lam).T
            y = o * lax.rsqrt(jnp.mean(o * o, axis=-1, keepdims=True) + RMS_EPS)
            o_ref[pl.ds(pl.multiple_of(cur[1] * tq, tq), tq), cur[0] * LANES:(cur[0] + 1) * LANES] = (
                y * g_ref[...] * (1.0 - lam_init)).astype(o_ref.dtype)
        return tuple(mx)

    bufs = (sa_ref, sb_ref)
    trips = (nq - 1) // 2
    mx = step((0, 0, bufs[0]), None)
    for hh in range(hps):
        def trip(t, mx, hh=hh):
            for u in range(2):
                i = 2 * t + u
                mx = step((hh, i + 1, bufs[(u + 1) % 2]), (hh, i, bufs[u % 2], mx))
            return mx

        mx = lax.fori_loop(0, trips, trip, mx)
        for i in range(2 * trips, nq - 1):
            mx = step((hh, i + 1, bufs[(i + 1) % 2]), (hh, i, bufs[i % 2], mx))
        following = (hh + 1, 0, bufs[0]) if hh + 1 < hps else None
        mx = step(following, (hh, nq - 1, bufs[(nq - 1) % 2], mx))


def _diff_attn(proj3, da_lambda, da_norm_g, layer, tq, kb):
    B, S, _ = proj3.shape
    hps = ATTN_HEADS_PER_STEP
    lam_init = 0.8 - 0.6 * math.exp(-0.3 * layer)
    head = lambda base, j: pl.BlockSpec((None, S, LANES), lambda b, g: (b, 0, base + hps * g + j))
    return pl.pallas_call(
        functools.partial(_attn_kernel, lam_init=lam_init, tq=tq, kb=kb),
        grid=(B, DA_HEADS // hps),
        in_specs=([pl.BlockSpec((4, DA_HEAD_DIM), lambda b, g: (0, 0)),
                   pl.BlockSpec((1, LANES), lambda b, g: (0, 0))]
                  + [head(0, j) for j in range(hps)]
                  + [head(DA_HEADS, j) for j in range(hps)]
                  + [head(2 * DA_HEADS, j) for j in range(hps)]),
        out_specs=pl.BlockSpec((None, S, hps * LANES), lambda b, g: (b, 0, g)),
        out_shape=jax.ShapeDtypeStruct((B, S, DA_HEADS * LANES), BF16),
        scratch_shapes=[pltpu.VMEM((hps, LANES, S), BF16),
                        pltpu.VMEM((2, S, tq), F32),
                        pltpu.VMEM((2, S, tq), F32)],
        compiler_params=_cparams(("parallel", "parallel")),
        name="diff_attn",
    )(da_lambda, da_norm_g.reshape(1, LANES), *([proj3] * (3 * hps)))


def _bcast_rows(b, idxs, rep):
    return jnp.concatenate(
        [jnp.broadcast_to(b[r:r + 1, :], (rep, b.shape[1])) for r in idxs], axis=0)


def _ref_rows(b, m, reverse, row):
    C = b.shape[0]
    off = m if reverse else m - 1
    if m >= 4:
        return _bcast_rows(b, [G * 2 * m + off for G in range(C // (2 * m))], 2 * m)
    lo = _bcast_rows(b, [8 * G + off for G in range(C // 8)], 8)
    hi = _bcast_rows(b, [8 * G + 4 + off for G in range(C // 8)], 8)
    return jnp.where((row & 7) < 4, lo, hi)


HG_BLOCK = 32
HG_BLOCK_RANGE = 64.0


def _level_masks(C, reverse, block):
    row = lax.broadcasted_iota(jnp.int32, (C, C), 0)
    col = lax.broadcasted_iota(jnp.int32, (C, C), 1)
    masks = {}
    if block is None:
        masks["diag"] = jnp.where(row == col, 1.0, 0.0)
    else:
        shift = block.bit_length() - 1
        causal = (col >= row) if reverse else (col <= row)
        masks["block"] = jnp.where(((row >> shift) == (col >> shift)) & causal, 1.0, 0.0)
    m = 1 if block is None else block
    while m < C:
        shift = m.bit_length() - 1
        q_bit, k_bit = (0, 1) if reverse else (1, 0)
        keep = (((row >> (shift + 1)) == (col >> (shift + 1)))
                & (((row >> shift) & 1) == q_bit) & (((col >> shift) & 1) == k_bit))
        masks[m] = jnp.where(keep, 1.0, 0.0)
        m *= 2
    return masks


def _hgrn_chunk(q, g, v, tri, masks, reverse, block):
    C = q.shape[0]
    row = lax.broadcasted_iota(jnp.int32, (C, LANES), 0)
    f = jnp.exp(g)
    k = (1.0 - f).astype(BF16)

    g1 = g.astype(BF16)
    r1 = g - g1.astype(F32)
    g2 = r1.astype(BF16)
    g3 = (r1 - g2.astype(F32)).astype(BF16)
    cs = jnp.dot(tri, jnp.concatenate([g1, g2, g3], axis=1), preferred_element_type=F32)
    yield None

    b = cs[:, :LANES] + cs[:, LANES:2 * LANES] + cs[:, 2 * LANES:]
    edge = 0 if reverse else C - 1
    b_edge = b[edge:edge + 1, :]
    q_hat = q * jnp.exp2(b * LOG2E).astype(BF16)
    k_hat = k * jnp.exp2((b_edge - b) * LOG2E).astype(BF16)
    upd = lax.dot_general(v, k_hat, TN_DIMS, preferred_element_type=F32)

    scores = []
    if block is None:
        scores.append((lax.dot_general(q, k, NT_DIMS, preferred_element_type=F32), masks["diag"]))
    else:
        off = 0 if reverse else block - 1
        d = (b - _bcast_rows(b, [G * block + off for G in range(C // block)], block)) * LOG2E
        scores.append((lax.dot_general(q * jnp.exp2(d).astype(BF16), k * jnp.exp2(-d).astype(BF16),
                                       NT_DIMS, preferred_element_type=F32), masks["block"]))
    m = 1 if block is None else block
    while m < C:
        if m == 1:
            is_query = (row & 1) == (0 if reverse else 1)
            e = jnp.where(is_query, f, 1.0).astype(BF16)
        else:
            d = b - _ref_rows(b, m, reverse, row)
            e = jnp.exp2(jnp.abs(d) * (-LOG2E)).astype(BF16)
        scores.append((lax.dot_general(q * e, k * e, NT_DIMS, preferred_element_type=F32), masks[m]))
        m *= 2
    state_t = yield None

    attn = scores[0][0] * scores[0][1]
    for s, mask in scores[1:]:
        attn = attn + s * mask
    o = (jnp.dot(attn.astype(BF16), v, preferred_element_type=F32)
         + lax.dot_general(q_hat, state_t.astype(BF16), NT_DIMS, preferred_element_type=F32))
    new_state = state_t * jnp.exp(b_edge) + upd
    yield o, new_state


def _hgrn_kernel(q_ref, gf_ref, gb_ref, v_ref, of_ref, ob_ref):
    S = q_ref.shape[0]
    C = HG_CHUNK
    n = S // C
    row = lax.broadcasted_iota(jnp.int32, (C, C), 0)
    col = lax.broadcasted_iota(jnp.int32, (C, C), 1)
    tri_f = jnp.where(col <= row, 1.0, 0.0).astype(BF16)
    tri_b = jnp.where(col >= row, 1.0, 0.0).astype(BF16)
    per_trip = math.gcd(n, 4)

    def scan(block):
        masks_f = _level_masks(C, False, block)
        masks_b = _level_masks(C, True, block)

        def body(t, carry):
            states = list(carry)
            chains = []
            for u in range(per_trip):
                cf = t * per_trip + u
                for rev, g_ref, tri, masks in ((0, gf_ref, tri_f, masks_f), (1, gb_ref, tri_b, masks_b)):
                    rows = pl.ds(pl.multiple_of((n - 1 - cf if rev else cf) * C, C), C)
                    chains.append((rev, rows, _hgrn_chunk(q_ref[rows, :], g_ref[rows, :], v_ref[rows, :],
                                                          tri, masks, bool(rev), block)))
            for _ in range(2):
                for _, _, gen in chains:
                    next(gen)
            for rev, rows, gen in chains:
                o, states[rev] = gen.send(states[rev])
                (ob_ref if rev else of_ref)[rows, :] = o
            return tuple(states)

        z = jnp.zeros((LANES, LANES), F32)
        lax.fori_loop(0, n // per_trip, body, (z, z))

    def min_block_sum(g_ref):
        tot = jnp.sum(g_ref[...].reshape(S // HG_BLOCK, HG_BLOCK, LANES), axis=1)
        return jnp.min(tot)

    mild = jnp.minimum(min_block_sum(gf_ref), min_block_sum(gb_ref)) >= -HG_BLOCK_RANGE
    pl.when(mild)(functools.partial(scan, HG_BLOCK))
    pl.when(jnp.logical_not(mild))(functools.partial(scan, None))


def _hgrn(pa3, pg3, pc3):
    B, S, _ = pa3.shape
    spec = lambda blk: pl.BlockSpec((None, S, LANES), lambda b, h: (b, 0, blk + h))
    out = jax.ShapeDtypeStruct((B, S, HG_HEADS * LANES), F32)
    return pl.pallas_call(
        _hgrn_kernel,
        grid=(B, HG_HEADS),
        in_specs=[spec(12), spec(0), spec(4), spec(0)],
        out_specs=[pl.BlockSpec((None, S, LANES), lambda b, h: (b, 0, h)),
                   pl.BlockSpec((None, S, LANES), lambda b, h: (b, 0, h))],
        out_shape=[out, out],
        compiler_params=_cparams(("parallel", "parallel")),
        name="hgrn2",
    )(pa3, pg3, pg3, pc3)


def _mm_kernel(x_ref, w_ref, o_ref):
    o_ref[...] = jnp.dot(x_ref[...].astype(BF16), w_ref[...],
                         preferred_element_type=F32).astype(o_ref.dtype)


def _matmul(x, w, layer, tm, tn):
    M, K = x.shape
    N = w.shape[2]
    return pl.pallas_call(
        _mm_kernel,
        grid=(M // tm, N // tn),
        in_specs=[pl.BlockSpec((tm, K), lambda i, j: (i, 0)),
                  pl.BlockSpec((None, K, tn), lambda i, j: (layer, 0, j))],
        out_specs=pl.BlockSpec((tm, tn), lambda i, j: (i, j)),
        out_shape=jax.ShapeDtypeStruct((M, N), BF16),
        compiler_params=_cparams(("parallel", "parallel")),
        name="mem_kv_proj",
    )(x, w)


def _outproj_kernel(da_ref, of_ref, ob_ref, gate_ref, hgn_ref, mq_ref, mkv_ref, w_ref, x_ref,
                    g_ref, b_ref, o_ref, obf_ref):
    hg = of_ref[...] + ob_ref[...]
    gate = gate_ref[...].astype(F32)
    parts = [da_ref[...]]
    for h in range(HG_HEADS):
        sl = slice(h * LANES, (h + 1) * LANES)
        y = hg[:, sl]
        y = y * lax.rsqrt(jnp.mean(y * y, axis=-1, keepdims=True) + RMS_EPS) * hgn_ref[...]
        parts.append((y * gate[:, sl]).astype(BF16))
    width = MX_HEADS * LANES
    for h in range(MX_HEADS):
        sl = slice(h * LANES, (h + 1) * LANES)
        s = lax.dot_general(mq_ref[:, sl], mkv_ref[:, sl], NT_DIMS, preferred_element_type=F32)
        p = jnp.exp(s - jnp.max(s, axis=-1, keepdims=True))
        p = p * (1.0 / jnp.sum(p, axis=-1, keepdims=True))
        mv = mkv_ref[:, width + h * LANES:width + (h + 1) * LANES]
        parts.append(jnp.dot(p.astype(BF16), mv, preferred_element_type=F32).astype(BF16))
    mix = jnp.dot(jnp.concatenate(parts, axis=1), w_ref[...], preferred_element_type=F32)
    y = _layer_norm_rows(DEEPNORM_ALPHA * x_ref[...] + mix, g_ref[...], b_ref[...])
    o_ref[...] = y
    obf_ref[...] = y.astype(BF16)


def _outproj(da_o, o_f, o_b, pc, hg_norm_g, mkv3, w_out, layer, x, g, b, tm, seq):
    T, D = x.shape
    W = SEC
    M = mkv3.shape[1]
    per_seq = seq // tm
    row = lambda i: (i, 0)
    const = lambda i: (0, 0)
    return pl.pallas_call(
        _outproj_kernel,
        grid=(T // tm,),
        in_specs=[pl.BlockSpec((tm, W), row), pl.BlockSpec((tm, W), row), pl.BlockSpec((tm, W), row),
                  pl.BlockSpec((tm, W), lambda i: (i, 1)),
                  pl.BlockSpec((1, LANES), const),
                  pl.BlockSpec((tm, W), lambda i: (i, 2)),
                  pl.BlockSpec((None, M, 2 * W), lambda i: (i // per_seq, 0, 0)),
                  pl.BlockSpec((None, 3 * W, D), lambda i: (layer, 0, 0)),
                  pl.BlockSpec((tm, D), row),
                  pl.BlockSpec((1, D), const), pl.BlockSpec((1, D), const)],
        out_specs=[pl.BlockSpec((tm, D), row), pl.BlockSpec((tm, D), row)],
        out_shape=[jax.ShapeDtypeStruct((T, D), F32), jax.ShapeDtypeStruct((T, D), BF16)],
        compiler_params=_cparams(("parallel",)),
        name="outproj_ln",
    )(da_o, o_f, o_b, pc, hg_norm_g.reshape(1, LANES), pc, mkv3, w_out, x, g.reshape(1, D),
      b.reshape(1, D))


FF_TILE = 256
HALO = 16


def _ffn_kernel(xp_ref, xm_ref, xn_ref, xr_ref, wu_ref, cw_ref, cb_ref, wd_ref, g_ref, b_ref,
                o_ref, obf_ref, xcat_ref, act_ref, *, tiles_per_seq):
    i = pl.program_id(0)
    tm = xm_ref.shape[0]
    first = (i % tiles_per_seq) == 0
    last = (i % tiles_per_seq) == tiles_per_seq - 1
    xcat_ref[0:HALO, :] = jnp.where(first, jnp.zeros_like(xp_ref[...]), xp_ref[...])
    xcat_ref[HALO:HALO + tm, :] = xm_ref[...]
    xcat_ref[HALO + tm:, :] = jnp.where(last, jnp.zeros_like(xn_ref[...]), xn_ref[...])
    xc = xcat_ref[...]

    for j in range(D_FF // FF_TILE):
        halves = []
        for part, base in enumerate((0, D_FF)):
            cols = slice(base + j * FF_TILE, base + (j + 1) * FF_TILE)
            u = jnp.dot(xc, wu_ref[:, cols], preferred_element_type=F32)
            rows = u.shape[0]
            cw = cw_ref[:, cols]
            halves.append(pltpu.roll(u, 1, axis=0)[HALO:HALO + tm, :] * cw[0:1, :]
                          + u[HALO:HALO + tm, :] * cw[1:2, :]
                          + pltpu.roll(u, rows - 1, axis=0)[HALO:HALO + tm, :] * cw[2:3, :]
                          + cb_ref[:, cols])
        gate, val = halves
        act_ref[:, j * FF_TILE:(j + 1) * FF_TILE] = (gate * _sigmoid(gate) * val).astype(BF16)

    ffn = jnp.dot(act_ref[...], wd_ref[...], preferred_element_type=F32)
    y = _layer_norm_rows(DEEPNORM_ALPHA * xr_ref[...] + ffn, g_ref[...], b_ref[...])
    o_ref[...] = y
    obf_ref[...] = y.astype(BF16)


def _ffn(xb, x, w_up, conv_w, conv_b, w_down, layer, g, b, tm, seq):
    T, D = x.shape
    hb = tm // HALO
    nhb = T // HALO
    row = lambda i: (i, 0)
    const = lambda i: (0, 0)
    resident = lambda shape: pl.BlockSpec(shape, const, pipeline_mode=pl.Buffered(1))
    stacked = lambda shape: pl.BlockSpec((None,) + shape, lambda i: (layer, 0, 0),
                                         pipeline_mode=pl.Buffered(1))
    return pl.pallas_call(
        functools.partial(_ffn_kernel, tiles_per_seq=seq // tm),
        grid=(T // tm,),
        in_specs=[pl.BlockSpec((HALO, D), lambda i: (jnp.maximum(i * hb - 1, 0), 0)),
                  pl.BlockSpec((tm, D), row),
                  pl.BlockSpec((HALO, D), lambda i: (jnp.minimum((i + 1) * hb, nhb - 1), 0)),
                  pl.BlockSpec((tm, D), row),
                  stacked((D, 2 * D_FF)),
                  resident((3, 2 * D_FF)),
                  resident((1, 2 * D_FF)),
                  stacked((D_FF, D)),
                  pl.BlockSpec((1, D), const), pl.BlockSpec((1, D), const)],
        out_specs=[pl.BlockSpec((tm, D), row), pl.BlockSpec((tm, D), row)],
        out_shape=[jax.ShapeDtypeStruct((T, D), F32), jax.ShapeDtypeStruct((T, D), BF16)],
        scratch_shapes=[pltpu.VMEM((tm + 2 * HALO, D), BF16),
                        pltpu.VMEM((tm, D_FF), BF16)],
        compiler_params=_cparams(("parallel",)),
        name="conv_ffn_ln",
    )(xb, xb, xb, x, w_up, conv_w, conv_b.reshape(1, -1), w_down, g.reshape(1, D), b.reshape(1, D))


def _rope_tables(positions):
    half = ROPE_DIM // 2
    inv_freq = ROPE_THETA ** (-jnp.arange(0, ROPE_DIM, 2, dtype=F32) / ROPE_DIM)
    ang = positions.astype(F32).reshape(-1, 1) * inv_freq[None, :]
    j = jnp.arange(LANES) % DA_HEAD_DIM
    rotated = (j < ROPE_DIM)[None, :]
    sign = jnp.where(j < half, -1.0, 1.0).astype(F32)[None, :]
    reps = LANES // half
    cosf = jnp.where(rotated, jnp.tile(jnp.cos(ang), (1, reps)), 1.0)
    sinf = jnp.where(rotated, jnp.tile(jnp.sin(ang), (1, reps)) * sign, 0.0)
    return cosf, sinf


def _pick(n, target):
    t = min(n, target)
    while n % t:
        t //= 2
    return t


def kernel(x, mem, positions, ln_in_g, ln_in_b, w_in, da_lambda, da_norm_g, hg_lb_fwd, hg_lb_bwd,
           hg_norm_g, w_mem_kv, w_out, ln1_g, ln1_b, w_up, conv_w, conv_b, w_down, ln2_g, ln2_b):
    B, S, D = x.shape
    M = mem.shape[1]
    T = B * S
    tm = _pick(S, 1024)
    cosf, sinf = _rope_tables(positions)
    mem2 = mem.reshape(B * M, D)
    w_in_b, w_kv_b, w_out_b = w_in.astype(BF16), w_mem_kv.astype(BF16), w_out.astype(BF16)
    w_up_b, w_down_b = w_up.astype(BF16), w_down.astype(BF16)

    for l in range(DEPTH):
        if l == 0:
            pa, pg, pc, h = _inproj(x.reshape(T, D), w_in_b, cosf, sinf, hg_lb_fwd, hg_lb_bwd, l,
                                    _pick(S, 512), ln=(ln_in_g, ln_in_b))
        else:
            pa, pg, pc = _inproj(hb, w_in_b, cosf, sinf, hg_lb_fwd, hg_lb_bwd, l, tm)
        pa3 = pa.reshape(B, S, 4 * SEC)
        da_o = _diff_attn(pa3, da_lambda[l], da_norm_g[l], l, _pick(S, 256), _pick(S, 512))
        o_f, o_b = _hgrn(pa3, pg.reshape(B, S, 2 * SEC), pc.reshape(B, S, 3 * SEC))
        mkv = _matmul(mem2, w_kv_b, l, _pick(B * M, 512), 512)
        h, hb = _outproj(da_o.reshape(T, SEC), o_f.reshape(T, SEC), o_b.reshape(T, SEC), pc,
                         hg_norm_g[l], mkv.reshape(B, M, 2 * SEC), w_out_b, l, h, ln1_g[l],
                         ln1_b[l], _pick(S, 512), S)
        h, hb = _ffn(hb, h, w_up_b, conv_w[l], conv_b[l], w_down_b, l, ln2_g[l], ln2_b[l],
                     _pick(S, 512), S)
    return h.reshape(B, S, D)
```

```python
import functools
import math

import jax
import jax.numpy as jnp
from jax import lax
from jax.experimental import pallas as pl
from jax.experimental.pallas import tpu as pltpu

F32 = jnp.float32
BF16 = jnp.bfloat16

D_MODEL = 1024
DEPTH = 4
DA_HEADS = 4
DA_HEAD_DIM = 64
HG_HEADS = 4
HG_MIN_FORGET = 1e-20
MX_HEADS = 4
MX_HEAD_DIM = 128
ROPE_THETA = 500000.0
ROPE_DIM = DA_HEAD_DIM // 4
D_FF = 2816
LN_EPS = 1e-5
RMS_EPS = 1e-6
DEEPNORM_ALPHA = (2 * DEPTH) ** 0.25
LOG2E = 1.4426950408889634

LANES = 128
SEC = 512
N_SEC = 9
HG_CHUNK = 128
VMEM_LIMIT = 56 * 1024 * 1024

NT_DIMS = (((1,), (1,)), ((), ()))
TN_DIMS = (((0,), (0,)), ((), ()))


def _cparams(sem):
    return pltpu.CompilerParams(dimension_semantics=sem, vmem_limit_bytes=VMEM_LIMIT)


def _layer_norm_rows(y, g, b):
    mu = jnp.mean(y, axis=-1, keepdims=True)
    d = y - mu
    var = jnp.mean(d * d, axis=-1, keepdims=True)
    return d * lax.rsqrt(var + LN_EPS) * g + b


def _sigmoid(z):
    return 1.0 / (1.0 + jnp.exp(-z))


def _lower_bound(lb_ref, layer):
    z = lb_ref[...]
    e = jnp.exp(z - jnp.max(z, axis=0, keepdims=True))
    p = e / jnp.sum(e, axis=0, keepdims=True)
    lb = jnp.zeros((1, z.shape[1]), F32)
    for j in range(1, layer + 1):
        lb = lb + p[j:j + 1, :]
    return lb


def _inproj_kernel(*refs, layer, sub, norm_input):
    if norm_input:
        (xin_ref, lng_ref, lnb_ref, w_ref, cos_ref, sin_ref, lbf_ref, lbb_ref,
         a_ref, g_ref, c_ref, h_ref, x_ref) = refs
        h = _layer_norm_rows(xin_ref[...], lng_ref[...], lnb_ref[...])
        h_ref[...] = h
        x_ref[...] = h.astype(BF16)
    else:
        x_ref, w_ref, cos_ref, sin_ref, lbf_ref, lbb_ref, a_ref, g_ref, c_ref = refs
    tm = x_ref.shape[0]

    def rope(scale):
        def epilogue(acc, rows):
            cs = cos_ref[rows, :]
            sn = sin_ref[rows, :]
            lane = lax.broadcasted_iota(jnp.int32, cs.shape, 1)
            first = (lane & 15) < 8
            parts = []
            for j in range(SEC // LANES):
                t = acc[:, j * LANES:(j + 1) * LANES]
                partner = jnp.where(first, pltpu.roll(t, LANES - 8, axis=1), pltpu.roll(t, 8, axis=1))
                parts.append((t * cs + partner * sn) * scale)
            return jnp.concatenate(parts, axis=1)
        return epilogue

    def log_forget(lb_ref):
        lb = _lower_bound(lb_ref, layer)

        def epilogue(acc, rows):
            f = lb + (1.0 - lb) * _sigmoid(acc)
            return jnp.log(jnp.maximum(f, HG_MIN_FORGET))
        return epilogue

    identity = lambda acc, rows: acc
    silu = lambda acc, rows: acc * _sigmoid(acc)
    sections = [
        (rope(DA_HEAD_DIM ** -0.5 * LOG2E), a_ref, 0),
        (rope(1.0), a_ref, 1),
        (identity, a_ref, 2),
        (silu, a_ref, 3),
        (log_forget(lbf_ref), g_ref, 0),
        (log_forget(lbb_ref), g_ref, 1),
        (identity, c_ref, 0),
        (silu, c_ref, 1),
        (lambda acc, rows: acc * (MX_HEAD_DIM ** -0.5), c_ref, 2),
    ]
    for n, (epilogue, out_ref, slot) in enumerate(sections):
        for r in range(tm // sub):
            rows = slice(r * sub, (r + 1) * sub)
            acc = jnp.dot(x_ref[rows, :], w_ref[:, n * SEC:(n + 1) * SEC],
                          preferred_element_type=F32)
            out_ref[rows, slot * SEC:(slot + 1) * SEC] = epilogue(acc, rows).astype(out_ref.dtype)


def _inproj(x, w, cosf, sinf, lbf, lbb, layer, tm, ln=None):
    T, D = x.shape
    row = lambda i: (i, 0)
    const = lambda i: (0, 0)
    in_specs = [pl.BlockSpec((tm, D), row)]
    args = [x]
    out_specs = [pl.BlockSpec((tm, 4 * SEC), row),
                 pl.BlockSpec((tm, 2 * SEC), row),
                 pl.BlockSpec((tm, 3 * SEC), row)]
    out_shape = [jax.ShapeDtypeStruct((T, 4 * SEC), BF16),
                 jax.ShapeDtypeStruct((T, 2 * SEC), F32),
                 jax.ShapeDtypeStruct((T, 3 * SEC), BF16)]
    scratch = []
    if ln is not None:
        in_specs += [pl.BlockSpec((1, D), const), pl.BlockSpec((1, D), const)]
        args += [ln[0].reshape(1, D), ln[1].reshape(1, D)]
        out_specs.append(pl.BlockSpec((tm, D), row))
        out_shape.append(jax.ShapeDtypeStruct((T, D), F32))
        scratch.append(pltpu.VMEM((tm, D), BF16))
    in_specs += [pl.BlockSpec((None, D, N_SEC * SEC), lambda i: (layer, 0, 0),
                              pipeline_mode=pl.Buffered(1)),
                 pl.BlockSpec((tm, LANES), row),
                 pl.BlockSpec((tm, LANES), row),
                 pl.BlockSpec((DEPTH, SEC), const),
                 pl.BlockSpec((DEPTH, SEC), const)]
    args += [w, cosf, sinf, lbf, lbb]
    return pl.pallas_call(
        functools.partial(_inproj_kernel, layer=layer, sub=min(tm, 256),
                          norm_input=ln is not None),
        grid=(T // tm,),
        in_specs=in_specs,
        out_specs=out_specs,
        out_shape=out_shape,
        scratch_shapes=scratch,
        compiler_params=_cparams(("parallel",)),
        name="inproj",
    )(*args)


ATTN_HEADS_PER_STEP = 2


def _attn_kernel(lam_ref, g_ref, *refs, lam_init, tq, kb):
    hps = ATTN_HEADS_PER_STEP
    q_refs, k_refs, v_refs = refs[:hps], refs[hps:2 * hps], refs[2 * hps:3 * hps]
    o_ref, vt_ref, sa_ref, sb_ref = refs[3 * hps:]
    S = k_refs[0].shape[0]
    nq = S // tq
    for hh in range(hps):
        for j in range(S // kb):
            vt_ref[hh, :, j * kb:(j + 1) * kb] = (
                v_refs[hh][j * kb:(j + 1) * kb, :].astype(F32).T.astype(BF16))

    lf = lam_ref[...]
    lam = (jnp.exp(jnp.sum(lf[0:1, :] * lf[1:2, :], axis=-1, keepdims=True))
           - jnp.exp(jnp.sum(lf[2:3, :] * lf[3:4, :], axis=-1, keepdims=True)) + lam_init)
    lane = lax.broadcasted_iota(jnp.int32, (tq, LANES), 1)

    def step(nxt, cur):
        if nxt is not None:
            q = q_refs[nxt[0]][pl.ds(pl.multiple_of(nxt[1] * tq, tq), tq), :]
            kbf = k_refs[nxt[0]][...]
        mx, outs = [], []
        for c in range(2):
            if nxt is not None:
                keep = (lane < DA_HEAD_DIM) if c == 0 else (lane >= DA_HEAD_DIM)
                s = lax.dot_general(kbf, jnp.where(keep, q, jnp.zeros_like(q)), NT_DIMS,
                                    preferred_element_type=F32)
                nxt[2][c] = s
                mx.append(jnp.max(s, axis=0, keepdims=True))
            if cur is not None:
                p = jnp.exp2(cur[2][c] - cur[3][c])
                inv = 1.0 / jnp.sum(p, axis=0, keepdims=True)
                outs.append(jnp.dot(vt_ref[cur[0]], p.astype(BF16), preferred_element_type=F32) * inv)
        if cur is not None:
            o = (outs[0] - outs[1] * lam).T
            y = o * lax.rsqrt(jnp.mean(o * o, axis=-1, keepdims=True) + RMS_EPS)
            o_ref[pl.ds(pl.multiple_of(cur[1] * tq, tq), tq), cur[0] * LANES:(cur[0] + 1) * LANES] = (
                y * g_ref[...] * (1.0 - lam_init)).astype(o_ref.dtype)
        return tuple(mx)

    bufs = (sa_ref, sb_ref)
    trips = (nq - 1) // 2
    mx = step((0, 0, bufs[0]), None)
    for hh in range(hps):
        def trip(t, mx, hh=hh):
            for u in range(2):
                i = 2 * t + u
                mx = step((hh, i + 1, bufs[(u + 1) % 2]), (hh, i, bufs[u % 2], mx))
            return mx

        mx = lax.fori_loop(0, trips, trip, mx)
        for i in range(2 * trips, nq - 1):
            mx = step((hh, i + 1, bufs[(i + 1) % 2]), (hh, i, bufs[i % 2], mx))
        following = (hh + 1, 0, bufs[0]) if hh + 1 < hps else None
        mx = step(following, (hh, nq - 1, bufs[(nq - 1) % 2], mx))


def _diff_attn(proj3, da_lambda, da_norm_g, layer, tq, kb):
    B, S, _ = proj3.shape
    hps = ATTN_HEADS_PER_STEP
    lam_init = 0.8 - 0.6 * math.exp(-0.3 * layer)
    head = lambda base, j: pl.BlockSpec((None, S, LANES), lambda b, g: (b, 0, base + hps * g + j))
    return pl.pallas_call(
        functools.partial(_attn_kernel, lam_init=lam_init, tq=tq, kb=kb),
        grid=(B, DA_HEADS // hps),
        in_specs=([pl.BlockSpec((4, DA_HEAD_DIM), lambda b, g: (0, 0)),
                   pl.BlockSpec((1, LANES), lambda b, g: (0, 0))]
                  + [head(0, j) for j in range(hps)]
                  + [head(DA_HEADS, j) for j in range(hps)]
                  + [head(2 * DA_HEADS, j) for j in range(hps)]),
        out_specs=pl.BlockSpec((None, S, hps * LANES), lambda b, g: (b, 0, g)),
        out_shape=jax.ShapeDtypeStruct((B, S, DA_HEADS * LANES), BF16),
        scratch_shapes=[pltpu.VMEM((hps, LANES, S), BF16),
                        pltpu.VMEM((2, S, tq), F32),
                        pltpu.VMEM((2, S, tq), F32)],
        compiler_params=_cparams(("parallel", "parallel")),
        name="diff_attn",
    )(da_lambda, da_norm_g.reshape(1, LANES), *([proj3] * (3 * hps)))


def _bcast_rows(b, idxs, rep):
    return jnp.concatenate(
        [jnp.broadcast_to(b[r:r + 1, :], (rep, b.shape[1])) for r in idxs], axis=0)


def _ref_rows(b, m, reverse, row):
    C = b.shape[0]
    off = m if reverse else m - 1
    if m >= 4:
        return _bcast_rows(b, [G * 2 * m + off for G in range(C // (2 * m))], 2 * m)
    lo = _bcast_rows(b, [8 * G + off for G in range(C // 8)], 8)
    hi = _bcast_rows(b, [8 * G + 4 + off for G in range(C // 8)], 8)
    return jnp.where((row & 7) < 4, lo, hi)


HG_BLOCK = 32
HG_BLOCK_RANGE = 64.0


def _level_masks(C, reverse, block):
    row = lax.broadcasted_iota(jnp.int32, (C, C), 0)
    col = lax.broadcasted_iota(jnp.int32, (C, C), 1)
    masks = {}
    if block is None:
        masks["diag"] = jnp.where(row == col, 1.0, 0.0)
    else:
        shift = block.bit_length() - 1
        causal = (col >= row) if reverse else (col <= row)
        masks["block"] = jnp.where(((row >> shift) == (col >> shift)) & causal, 1.0, 0.0)
    m = 1 if block is None else block
    while m < C:
        shift = m.bit_length() - 1
        q_bit, k_bit = (0, 1) if reverse else (1, 0)
        keep = (((row >> (shift + 1)) == (col >> (shift + 1)))
                & (((row >> shift) & 1) == q_bit) & (((col >> shift) & 1) == k_bit))
        masks[m] = jnp.where(keep, 1.0, 0.0)
        m *= 2
    return masks


def _hgrn_chunk(q, g, v, tri, masks, reverse, block):
    C = q.shape[0]
    row = lax.broadcasted_iota(jnp.int32, (C, LANES), 0)
    f = jnp.exp(g)
    k = (1.0 - f).astype(BF16)

    g1 = g.astype(BF16)
    r1 = g - g1.astype(F32)
    g2 = r1.astype(BF16)
    g3 = (r1 - g2.astype(F32)).astype(BF16)
    cs = jnp.dot(tri, jnp.concatenate([g1, g2, g3], axis=1), preferred_element_type=F32)
    yield None

    b = cs[:, :LANES] + cs[:, LANES:2 * LANES] + cs[:, 2 * LANES:]
    edge = 0 if reverse else C - 1
    b_edge = b[edge:edge + 1, :]
    q_hat = q * jnp.exp2(b * LOG2E).astype(BF16)
    k_hat = k * jnp.exp2((b_edge - b) * LOG2E).astype(BF16)
    upd = lax.dot_general(v, k_hat, TN_DIMS, preferred_element_type=F32)

    scores = []
    if block is None:
        scores.append((lax.dot_general(q, k, NT_DIMS, preferred_element_type=F32), masks["diag"]))
    else:
        off = 0 if reverse else block - 1
        d = (b - _bcast_rows(b, [G * block + off for G in range(C // block)], block)) * LOG2E
        scores.append((lax.dot_general(q * jnp.exp2(d).astype(BF16), k * jnp.exp2(-d).astype(BF16),
                                       NT_DIMS, preferred_element_type=F32), masks["block"]))
    m = 1 if block is None else block
    while m < C:
        if m == 1:
            is_query = (row & 1) == (0 if reverse else 1)
            e = jnp.where(is_query, f, 1.0).astype(BF16)
        else:
            d = b - _ref_rows(b, m, reverse, row)
            e = jnp.exp2(jnp.abs(d) * (-LOG2E)).astype(BF16)
        scores.append((lax.dot_general(q * e, k * e, NT_DIMS, preferred_element_type=F32), masks[m]))
        m *= 2
    state_t = yield None

    attn = scores[0][0] * scores[0][1]
    for s, mask in scores[1:]:
        attn = attn + s * mask
    o = (jnp.dot(attn.astype(BF16), v, preferred_element_type=F32)
         + lax.dot_general(q_hat, state_t.astype(BF16), NT_DIMS, preferred_element_type=F32))
    new_state = state_t * jnp.exp(b_edge) + upd
    yield o, new_state


def _hgrn_kernel(q_ref, gf_ref, gb_ref, v_ref, of_ref, ob_ref):
    S = q_ref.shape[0]
    C = HG_CHUNK
    n = S // C
    row = lax.broadcasted_iota(jnp.int32, (C, C), 0)
    col = lax.broadcasted_iota(jnp.int32, (C, C), 1)
    tri_f = jnp.where(col <= row, 1.0, 0.0).astype(BF16)
    tri_b = jnp.where(col >= row, 1.0, 0.0).astype(BF16)
    per_trip = math.gcd(n, 8)

    def scan(block):
        masks_f = _level_masks(C, False, block)
        masks_b = _level_masks(C, True, block)

        def body(t, carry):
            states = list(carry)
            chains = []
            for u in range(per_trip):
                cf = t * per_trip + u
                for rev, g_ref, tri, masks in ((0, gf_ref, tri_f, masks_f), (1, gb_ref, tri_b, masks_b)):
                    rows = pl.ds(pl.multiple_of((n - 1 - cf if rev else cf) * C, C), C)
                    chains.append((rev, rows, _hgrn_chunk(q_ref[rows, :], g_ref[rows, :], v_ref[rows, :],
                                                          tri, masks, bool(rev), block)))
            for _ in range(2):
                for _, _, gen in chains:
                    next(gen)
            for rev, rows, gen in chains:
                o, states[rev] = gen.send(states[rev])
                (ob_ref if rev else of_ref)[rows, :] = o
            return tuple(states)

        z = jnp.zeros((LANES, LANES), F32)
        lax.fori_loop(0, n // per_trip, body, (z, z))

    def min_block_sum(g_ref):
        tot = jnp.sum(g_ref[...].reshape(S // HG_BLOCK, HG_BLOCK, LANES), axis=1)
        return jnp.min(tot)

    mild = jnp.minimum(min_block_sum(gf_ref), min_block_sum(gb_ref)) >= -HG_BLOCK_RANGE
    pl.when(mild)(functools.partial(scan, HG_BLOCK))
    pl.when(jnp.logical_not(mild))(functools.partial(scan, None))


def _hgrn(pa3, pg3, pc3):
    B, S, _ = pa3.shape
    spec = lambda blk: pl.BlockSpec((None, S, LANES), lambda b, h: (b, 0, blk + h))
    out = jax.ShapeDtypeStruct((B, S, HG_HEADS * LANES), F32)
    return pl.pallas_call(
        _hgrn_kernel,
        grid=(B, HG_HEADS),
        in_specs=[spec(12), spec(0), spec(4), spec(0)],
        out_specs=[pl.BlockSpec((None, S, LANES), lambda b, h: (b, 0, h)),
                   pl.BlockSpec((None, S, LANES), lambda b, h: (b, 0, h))],
        out_shape=[out, out],
        compiler_params=_cparams(("parallel", "parallel")),
        name="hgrn2",
    )(pa3, pg3, pg3, pc3)


def _mm_kernel(x_ref, w_ref, o_ref):
    o_ref[...] = jnp.dot(x_ref[...].astype(BF16), w_ref[...],
                         preferred_element_type=F32).astype(o_ref.dtype)


def _matmul(x, w, layer, tm, tn):
    M, K = x.shape
    N = w.shape[2]
    return pl.pallas_call(
        _mm_kernel,
        grid=(M // tm, N // tn),
        in_specs=[pl.BlockSpec((tm, K), lambda i, j: (i, 0)),
                  pl.BlockSpec((None, K, tn), lambda i, j: (layer, 0, j))],
        out_specs=pl.BlockSpec((tm, tn), lambda i, j: (i, j)),
        out_shape=jax.ShapeDtypeStruct((M, N), BF16),
        compiler_params=_cparams(("parallel", "parallel")),
        name="mem_kv_proj",
    )(x, w)


def _outproj_kernel(da_ref, of_ref, ob_ref, gate_ref, hgn_ref, mq_ref, mkv_ref, w_ref, x_ref,
                    g_ref, b_ref, o_ref, obf_ref):
    hg = of_ref[...] + ob_ref[...]
    gate = gate_ref[...].astype(F32)
    parts = [da_ref[...]]
    for h in range(HG_HEADS):
        sl = slice(h * LANES, (h + 1) * LANES)
        y = hg[:, sl]
        y = y * lax.rsqrt(jnp.mean(y * y, axis=-1, keepdims=True) + RMS_EPS) * hgn_ref[...]
        parts.append((y * gate[:, sl]).astype(BF16))
    width = MX_HEADS * LANES
    for h in range(MX_HEADS):
        sl = slice(h * LANES, (h + 1) * LANES)
        s = lax.dot_general(mq_ref[:, sl], mkv_ref[:, sl], NT_DIMS, preferred_element_type=F32)
        p = jnp.exp(s - jnp.max(s, axis=-1, keepdims=True))
        p = p * (1.0 / jnp.sum(p, axis=-1, keepdims=True))
        mv = mkv_ref[:, width + h * LANES:width + (h + 1) * LANES]
        parts.append(jnp.dot(p.astype(BF16), mv, preferred_element_type=F32).astype(BF16))
    mix = jnp.dot(jnp.concatenate(parts, axis=1), w_ref[...], preferred_element_type=F32)
    y = _layer_norm_rows(DEEPNORM_ALPHA * x_ref[...] + mix, g_ref[...], b_ref[...])
    o_ref[...] = y
    obf_ref[...] = y.astype(BF16)


def _outproj(da_o, o_f, o_b, pc, hg_norm_g, mkv3, w_out, layer, x, g, b, tm, seq):
    T, D = x.shape
    W = SEC
    M = mkv3.shape[1]
    per_seq = seq // tm
    row = lambda i: (i, 0)
    const = lambda i: (0, 0)
    return pl.pallas_call(
        _outproj_kernel,
        grid=(T // tm,),
        in_specs=[pl.BlockSpec((tm, W), row), pl.BlockSpec((tm, W), row), pl.BlockSpec((tm, W), row),
                  pl.BlockSpec((tm, W), lambda i: (i, 1)),
                  pl.BlockSpec((1, LANES), const),
                  pl.BlockSpec((tm, W), lambda i: (i, 2)),
                  pl.BlockSpec((None, M, 2 * W), lambda i: (i // per_seq, 0, 0)),
                  pl.BlockSpec((None, 3 * W, D), lambda i: (layer, 0, 0)),
                  pl.BlockSpec((tm, D), row),
                  pl.BlockSpec((1, D), const), pl.BlockSpec((1, D), const)],
        out_specs=[pl.BlockSpec((tm, D), row), pl.BlockSpec((tm, D), row)],
        out_shape=[jax.ShapeDtypeStruct((T, D), F32), jax.ShapeDtypeStruct((T, D), BF16)],
        compiler_params=_cparams(("parallel",)),
        name="outproj_ln",
    )(da_o, o_f, o_b, pc, hg_norm_g.reshape(1, LANES), pc, mkv3, w_out, x, g.reshape(1, D),
      b.reshape(1, D))


FF_TILE = 256
HALO = 16


def _ffn_kernel(xp_ref, xm_ref, xn_ref, xr_ref, wu_ref, cw_ref, cb_ref, wd_ref, g_ref, b_ref,
                o_ref, obf_ref, xcat_ref, act_ref, *, tiles_per_seq):
    i = pl.program_id(0)
    tm = xm_ref.shape[0]
    first = (i % tiles_per_seq) == 0
    last = (i % tiles_per_seq) == tiles_per_seq - 1
    xcat_ref[0:HALO, :] = jnp.where(first, jnp.zeros_like(xp_ref[...]), xp_ref[...])
    xcat_ref[HALO:HALO + tm, :] = xm_ref[...]
    xcat_ref[HALO + tm:, :] = jnp.where(last, jnp.zeros_like(xn_ref[...]), xn_ref[...])
    xc = xcat_ref[...]

    for j in range(D_FF // FF_TILE):
        halves = []
        for part, base in enumerate((0, D_FF)):
            cols = slice(base + j * FF_TILE, base + (j + 1) * FF_TILE)
            u = jnp.dot(xc, wu_ref[:, cols], preferred_element_type=F32)
            rows = u.shape[0]
            cw = cw_ref[:, cols]
            halves.append(pltpu.roll(u, 1, axis=0)[HALO:HALO + tm, :] * cw[0:1, :]
                          + u[HALO:HALO + tm, :] * cw[1:2, :]
                          + pltpu.roll(u, rows - 1, axis=0)[HALO:HALO + tm, :] * cw[2:3, :]
                          + cb_ref[:, cols])
        gate, val = halves
        act_ref[:, j * FF_TILE:(j + 1) * FF_TILE] = (gate * _sigmoid(gate) * val).astype(BF16)

    n_half = 2 if tm % 32 == 0 else 1
    for r in range(n_half):
        rows = slice(r * (tm // n_half), (r + 1) * (tm // n_half))
        ffn = jnp.dot(act_ref[rows, :], wd_ref[...], preferred_element_type=F32)
        y = _layer_norm_rows(DEEPNORM_ALPHA * xr_ref[rows, :] + ffn, g_ref[...], b_ref[...])
        o_ref[rows, :] = y
        obf_ref[rows, :] = y.astype(BF16)


def _ffn(xb, x, w_up, conv_w, conv_b, w_down, layer, g, b, tm, seq):
    T, D = x.shape
    hb = tm // HALO
    nhb = T // HALO
    row = lambda i: (i, 0)
    const = lambda i: (0, 0)
    resident = lambda shape: pl.BlockSpec(shape, const, pipeline_mode=pl.Buffered(1))
    stacked = lambda shape: pl.BlockSpec((None,) + shape, lambda i: (layer, 0, 0),
                                         pipeline_mode=pl.Buffered(1))
    return pl.pallas_call(
        functools.partial(_ffn_kernel, tiles_per_seq=seq // tm),
        grid=(T // tm,),
        in_specs=[pl.BlockSpec((HALO, D), lambda i: (jnp.maximum(i * hb - 1, 0), 0)),
                  pl.BlockSpec((tm, D), row),
                  pl.BlockSpec((HALO, D), lambda i: (jnp.minimum((i + 1) * hb, nhb - 1), 0)),
                  pl.BlockSpec((tm, D), row),
                  stacked((D, 2 * D_FF)),
                  resident((3, 2 * D_FF)),
                  resident((1, 2 * D_FF)),
                  stacked((D_FF, D)),
                  pl.BlockSpec((1, D), const), pl.BlockSpec((1, D), const)],
        out_specs=[pl.BlockSpec((tm, D), row), pl.BlockSpec((tm, D), row)],
        out_shape=[jax.ShapeDtypeStruct((T, D), F32), jax.ShapeDtypeStruct((T, D), BF16)],
        scratch_shapes=[pltpu.VMEM((tm + 2 * HALO, D), BF16),
                        pltpu.VMEM((tm, D_FF), BF16)],
        compiler_params=_cparams(("parallel",)),
        name="conv_ffn_ln",
    )(xb, xb, xb, x, w_up, conv_w, conv_b.reshape(1, -1), w_down, g.reshape(1, D), b.reshape(1, D))


def _rope_tables(positions):
    half = ROPE_DIM // 2
    inv_freq = ROPE_THETA ** (-jnp.arange(0, ROPE_DIM, 2, dtype=F32) / ROPE_DIM)
    ang = positions.astype(F32).reshape(-1, 1) * inv_freq[None, :]
    j = jnp.arange(LANES) % DA_HEAD_DIM
    rotated = (j < ROPE_DIM)[None, :]
    sign = jnp.where(j < half, -1.0, 1.0).astype(F32)[None, :]
    reps = LANES // half
    cosf = jnp.where(rotated, jnp.tile(jnp.cos(ang), (1, reps)), 1.0)
    sinf = jnp.where(rotated, jnp.tile(jnp.sin(ang), (1, reps)) * sign, 0.0)
    return cosf, sinf


def _pick(n, target):
    t = min(n, target)
    while n % t:
        t //= 2
    return t


def kernel(x, mem, positions, ln_in_g, ln_in_b, w_in, da_lambda, da_norm_g, hg_lb_fwd, hg_lb_bwd,
           hg_norm_g, w_mem_kv, w_out, ln1_g, ln1_b, w_up, conv_w, conv_b, w_down, ln2_g, ln2_b):
    B, S, D = x.shape
    M = mem.shape[1]
    T = B * S
    tm = _pick(S, 1024)
    cosf, sinf = _rope_tables(positions)
    mem2 = mem.reshape(B * M, D)
    w_in_b, w_kv_b, w_out_b = w_in.astype(BF16), w_mem_kv.astype(BF16), w_out.astype(BF16)
    w_up_b, w_down_b = w_up.astype(BF16), w_down.astype(BF16)

    for l in range(DEPTH):
        if l == 0:
            pa, pg, pc, h = _inproj(x.reshape(T, D), w_in_b, cosf, sinf, hg_lb_fwd, hg_lb_bwd, l,
                                    _pick(S, 512), ln=(ln_in_g, ln_in_b))
        else:
            pa, pg, pc = _inproj(hb, w_in_b, cosf, sinf, hg_lb_fwd, hg_lb_bwd, l, tm)
        pa3 = pa.reshape(B, S, 4 * SEC)
        da_o = _diff_attn(pa3, da_lambda[l], da_norm_g[l], l, _pick(S, 256), _pick(S, 512))
        o_f, o_b = _hgrn(pa3, pg.reshape(B, S, 2 * SEC), pc.reshape(B, S, 3 * SEC))
        mkv = _matmul(mem2, w_kv_b, l, _pick(B * M, 512), 512)
        h, hb = _outproj(da_o.reshape(T, SEC), o_f.reshape(T, SEC), o_b.reshape(T, SEC), pc,
                         hg_norm_g[l], mkv.reshape(B, M, 2 * SEC), w_out_b, l, h, ln1_g[l],
                         ln1_b[l], _pick(S, 512), S)
        h, hb = _ffn(hb, h, w_up_b, conv_w[l], conv_b[l], w_down_b, l, ln2_g[l], ln2_b[l],
                     _pick(S, 512), S)
    return h.reshape(B, S, D)
```

```python
import functools
import math

import jax
import jax.numpy as jnp
from jax import lax
from jax.experimental import pallas as pl
from jax.experimental.pallas import tpu as pltpu

F32 = jnp.float32
BF16 = jnp.bfloat16

D_MODEL = 1024
DEPTH = 4
DA_HEADS = 4
DA_HEAD_DIM = 64
HG_HEADS = 4
HG_MIN_FORGET = 1e-20
MX_HEADS = 4
MX_HEAD_DIM = 128
ROPE_THETA = 500000.0
ROPE_DIM = DA_HEAD_DIM // 4
D_FF = 2816
LN_EPS = 1e-5
RMS_EPS = 1e-6
DEEPNORM_ALPHA = (2 * DEPTH) ** 0.25
LOG2E = 1.4426950408889634

LANES = 128
SEC = 512
N_SEC = 9
HG_CHUNK = 128
VMEM_LIMIT = 58 * 1024 * 1024

NT_DIMS = (((1,), (1,)), ((), ()))
TN_DIMS = (((0,), (0,)), ((), ()))


def _cparams(sem):
    return pltpu.CompilerParams(dimension_semantics=sem, vmem_limit_bytes=VMEM_LIMIT)


def _layer_norm_rows(y, g, b):
    mu = jnp.mean(y, axis=-1, keepdims=True)
    d = y - mu
    var = jnp.mean(d * d, axis=-1, keepdims=True)
    return d * lax.rsqrt(var + LN_EPS) * g + b


def _sigmoid(z):
    return 1.0 / (1.0 + jnp.exp(-z))


def _lower_bound(lb_ref, layer):
    z = lb_ref[...]
    e = jnp.exp(z - jnp.max(z, axis=0, keepdims=True))
    p = e / jnp.sum(e, axis=0, keepdims=True)
    lb = jnp.zeros((1, z.shape[1]), F32)
    for j in range(1, layer + 1):
        lb = lb + p[j:j + 1, :]
    return lb


def _inproj_kernel(*refs, layer, sub, norm_input):
    if norm_input:
        (xin_ref, lng_ref, lnb_ref, w_ref, cos_ref, sin_ref, lbf_ref, lbb_ref,
         a_ref, g_ref, c_ref, h_ref, x_ref) = refs
        h = _layer_norm_rows(xin_ref[...], lng_ref[...], lnb_ref[...])
        h_ref[...] = h
        x_ref[...] = h.astype(BF16)
    else:
        x_ref, w_ref, cos_ref, sin_ref, lbf_ref, lbb_ref, a_ref, g_ref, c_ref = refs
    tm = x_ref.shape[0]

    def rope(scale):
        def epilogue(acc, rows):
            cs = cos_ref[rows, :]
            sn = sin_ref[rows, :]
            lane = lax.broadcasted_iota(jnp.int32, cs.shape, 1)
            first = (lane & 15) < 8
            parts = []
            for j in range(SEC // LANES):
                t = acc[:, j * LANES:(j + 1) * LANES]
                partner = jnp.where(first, pltpu.roll(t, LANES - 8, axis=1), pltpu.roll(t, 8, axis=1))
                parts.append((t * cs + partner * sn) * scale)
            return jnp.concatenate(parts, axis=1)
        return epilogue

    def log_forget(lb_ref):
        lb = _lower_bound(lb_ref, layer)

        def epilogue(acc, rows):
            f = lb + (1.0 - lb) * _sigmoid(acc)
            return jnp.log(jnp.maximum(f, HG_MIN_FORGET))
        return epilogue

    identity = lambda acc, rows: acc
    silu = lambda acc, rows: acc * _sigmoid(acc)
    sections = [
        (rope(DA_HEAD_DIM ** -0.5 * LOG2E), a_ref, 0),
        (rope(1.0), a_ref, 1),
        (identity, a_ref, 2),
        (silu, a_ref, 3),
        (log_forget(lbf_ref), g_ref, 0),
        (log_forget(lbb_ref), g_ref, 1),
        (identity, c_ref, 0),
        (silu, c_ref, 1),
        (lambda acc, rows: acc * (MX_HEAD_DIM ** -0.5), c_ref, 2),
    ]
    for n, (epilogue, out_ref, slot) in enumerate(sections):
        for r in range(tm // sub):
            rows = slice(r * sub, (r + 1) * sub)
            acc = jnp.dot(x_ref[rows, :], w_ref[:, n * SEC:(n + 1) * SEC],
                          preferred_element_type=F32)
            out_ref[rows, slot * SEC:(slot + 1) * SEC] = epilogue(acc, rows).astype(out_ref.dtype)


def _inproj(x, w, cosf, sinf, lbf, lbb, layer, tm, ln=None):
    T, D = x.shape
    row = lambda i: (i, 0)
    const = lambda i: (0, 0)
    in_specs = [pl.BlockSpec((tm, D), row)]
    args = [x]
    out_specs = [pl.BlockSpec((tm, 4 * SEC), row),
                 pl.BlockSpec((tm, 2 * SEC), row),
                 pl.BlockSpec((tm, 3 * SEC), row)]
    out_shape = [jax.ShapeDtypeStruct((T, 4 * SEC), BF16),
                 jax.ShapeDtypeStruct((T, 2 * SEC), F32),
                 jax.ShapeDtypeStruct((T, 3 * SEC), BF16)]
    scratch = []
    if ln is not None:
        in_specs += [pl.BlockSpec((1, D), const), pl.BlockSpec((1, D), const)]
        args += [ln[0].reshape(1, D), ln[1].reshape(1, D)]
        out_specs.append(pl.BlockSpec((tm, D), row))
        out_shape.append(jax.ShapeDtypeStruct((T, D), F32))
        scratch.append(pltpu.VMEM((tm, D), BF16))
    in_specs += [pl.BlockSpec((None, D, N_SEC * SEC), lambda i: (layer, 0, 0),
                              pipeline_mode=pl.Buffered(1)),
                 pl.BlockSpec((tm, LANES), row),
                 pl.BlockSpec((tm, LANES), row),
                 pl.BlockSpec((DEPTH, SEC), const),
                 pl.BlockSpec((DEPTH, SEC), const)]
    args += [w, cosf, sinf, lbf, lbb]
    return pl.pallas_call(
        functools.partial(_inproj_kernel, layer=layer, sub=min(tm, 256),
                          norm_input=ln is not None),
        grid=(T // tm,),
        in_specs=in_specs,
        out_specs=out_specs,
        out_shape=out_shape,
        scratch_shapes=scratch,
        compiler_params=_cparams(("parallel",)),
        name="inproj",
    )(*args)


ATTN_HEADS_PER_STEP = 4


def _attn_kernel(lam_ref, g_ref, *refs, lam_init, tq, kb):
    hps = ATTN_HEADS_PER_STEP
    q_refs, k_refs, v_refs = refs[:hps], refs[hps:2 * hps], refs[2 * hps:3 * hps]
    o_ref, vt_ref, sa_ref, sb_ref = refs[3 * hps:]
    S = k_refs[0].shape[0]
    nq = S // tq
    for hh in range(hps):
        for j in range(S // kb):
            vt_ref[hh, :, j * kb:(j + 1) * kb] = (
                v_refs[hh][j * kb:(j + 1) * kb, :].astype(F32).T.astype(BF16))

    lf = lam_ref[...]
    lam = (jnp.exp(jnp.sum(lf[0:1, :] * lf[1:2, :], axis=-1, keepdims=True))
           - jnp.exp(jnp.sum(lf[2:3, :] * lf[3:4, :], axis=-1, keepdims=True)) + lam_init)
    lane = lax.broadcasted_iota(jnp.int32, (tq, LANES), 1)

    def step(nxt, cur):
        if nxt is not None:
            q = q_refs[nxt[0]][pl.ds(pl.multiple_of(nxt[1] * tq, tq), tq), :]
            kbf = k_refs[nxt[0]][...]
        mx, outs = [], []
        for c in range(2):
            if nxt is not None:
                keep = (lane < DA_HEAD_DIM) if c == 0 else (lane >= DA_HEAD_DIM)
                s = lax.dot_general(kbf, jnp.where(keep, q, jnp.zeros_like(q)), NT_DIMS,
                                    preferred_element_type=F32)
                nxt[2][c] = s
                mx.append(jnp.max(s, axis=0, keepdims=True))
            if cur is not None:
                p = jnp.exp2(cur[2][c] - cur[3][c])
                inv = 1.0 / jnp.sum(p, axis=0, keepdims=True)
                outs.append(jnp.dot(vt_ref[cur[0]], p.astype(BF16), preferred_element_type=F32) * inv)
        if cur is not None:
            o = (outs[0] - outs[1] * lam).T
            y = o * lax.rsqrt(jnp.mean(o * o, axis=-1, keepdims=True) + RMS_EPS)
            o_ref[pl.ds(pl.multiple_of(cur[1] * tq, tq), tq), cur[0] * LANES:(cur[0] + 1) * LANES] = (
                y * g_ref[...] * (1.0 - lam_init)).astype(o_ref.dtype)
        return tuple(mx)

    bufs = (sa_ref, sb_ref)
    trips = (nq - 1) // 2
    mx = step((0, 0, bufs[0]), None)
    for hh in range(hps):
        def trip(t, mx, hh=hh):
            for u in range(2):
                i = 2 * t + u
                mx = step((hh, i + 1, bufs[(u + 1) % 2]), (hh, i, bufs[u % 2], mx))
            return mx

        mx = lax.fori_loop(0, trips, trip, mx)
        for i in range(2 * trips, nq - 1):
            mx = step((hh, i + 1, bufs[(i + 1) % 2]), (hh, i, bufs[i % 2], mx))
        following = (hh + 1, 0, bufs[0]) if hh + 1 < hps else None
        mx = step(following, (hh, nq - 1, bufs[(nq - 1) % 2], mx))


def _diff_attn(proj3, da_lambda, da_norm_g, layer, tq, kb):
    B, S, _ = proj3.shape
    hps = ATTN_HEADS_PER_STEP
    lam_init = 0.8 - 0.6 * math.exp(-0.3 * layer)
    head = lambda base, j: pl.BlockSpec((None, S, LANES), lambda b, g: (b, 0, base + hps * g + j))
    return pl.pallas_call(
        functools.partial(_attn_kernel, lam_init=lam_init, tq=tq, kb=kb),
        grid=(B, DA_HEADS // hps),
        in_specs=([pl.BlockSpec((4, DA_HEAD_DIM), lambda b, g: (0, 0)),
                   pl.BlockSpec((1, LANES), lambda b, g: (0, 0))]
                  + [head(0, j) for j in range(hps)]
                  + [head(DA_HEADS, j) for j in range(hps)]
                  + [head(2 * DA_HEADS, j) for j in range(hps)]),
        out_specs=pl.BlockSpec((None, S, hps * LANES), lambda b, g: (b, 0, g)),
        out_shape=jax.ShapeDtypeStruct((B, S, DA_HEADS * LANES), BF16),
        scratch_shapes=[pltpu.VMEM((hps, LANES, S), BF16),
                        pltpu.VMEM((2, S, tq), F32),
                        pltpu.VMEM((2, S, tq), F32)],
        compiler_params=_cparams(("parallel", "parallel")),
        name="diff_attn",
    )(da_lambda, da_norm_g.reshape(1, LANES), *([proj3] * (3 * hps)))


def _bcast_rows(b, idxs, rep):
    return jnp.concatenate(
        [jnp.broadcast_to(b[r:r + 1, :], (rep, b.shape[1])) for r in idxs], axis=0)


def _ref_rows(b, m, reverse, row):
    C = b.shape[0]
    off = m if reverse else m - 1
    if m >= 4:
        return _bcast_rows(b, [G * 2 * m + off for G in range(C // (2 * m))], 2 * m)
    lo = _bcast_rows(b, [8 * G + off for G in range(C // 8)], 8)
    hi = _bcast_rows(b, [8 * G + 4 + off for G in range(C // 8)], 8)
    return jnp.where((row & 7) < 4, lo, hi)


HG_BLOCK = 32
HG_BLOCK_RANGE = 64.0


def _level_masks(C, reverse, block):
    row = lax.broadcasted_iota(jnp.int32, (C, C), 0)
    col = lax.broadcasted_iota(jnp.int32, (C, C), 1)
    masks = {}
    if block is None:
        masks["diag"] = jnp.where(row == col, 1.0, 0.0)
    else:
        shift = block.bit_length() - 1
        causal = (col >= row) if reverse else (col <= row)
        masks["block"] = jnp.where(((row >> shift) == (col >> shift)) & causal, 1.0, 0.0)
    m = 1 if block is None else block
    while m < C:
        shift = m.bit_length() - 1
        q_bit, k_bit = (0, 1) if reverse else (1, 0)
        keep = (((row >> (shift + 1)) == (col >> (shift + 1)))
                & (((row >> shift) & 1) == q_bit) & (((col >> shift) & 1) == k_bit))
        masks[m] = jnp.where(keep, 1.0, 0.0)
        m *= 2
    return masks


def _hgrn_chunk(q, g, v, tri, masks, reverse, block):
    C = q.shape[0]
    row = lax.broadcasted_iota(jnp.int32, (C, LANES), 0)
    f = jnp.exp(g)
    k = (1.0 - f).astype(BF16)

    g1 = g.astype(BF16)
    r1 = g - g1.astype(F32)
    g2 = r1.astype(BF16)
    g3 = (r1 - g2.astype(F32)).astype(BF16)
    cs = jnp.dot(tri, jnp.concatenate([g1, g2, g3], axis=1), preferred_element_type=F32)
    yield None

    b = cs[:, :LANES] + cs[:, LANES:2 * LANES] + cs[:, 2 * LANES:]
    edge = 0 if reverse else C - 1
    b_edge = b[edge:edge + 1, :]
    q_hat = q * jnp.exp2(b * LOG2E).astype(BF16)
    k_hat = k * jnp.exp2((b_edge - b) * LOG2E).astype(BF16)
    upd = lax.dot_general(v, k_hat, TN_DIMS, preferred_element_type=F32)

    scores = []
    if block is None:
        scores.append((lax.dot_general(q, k, NT_DIMS, preferred_element_type=F32), masks["diag"]))
    else:
        off = 0 if reverse else block - 1
        d = (b - _bcast_rows(b, [G * block + off for G in range(C // block)], block)) * LOG2E
        scores.append((lax.dot_general(q * jnp.exp2(d).astype(BF16), k * jnp.exp2(-d).astype(BF16),
                                       NT_DIMS, preferred_element_type=F32), masks["block"]))
    m = 1 if block is None else block
    while m < C:
        if m == 1:
            is_query = (row & 1) == (0 if reverse else 1)
            e = jnp.where(is_query, f, 1.0).astype(BF16)
        else:
            d = b - _ref_rows(b, m, reverse, row)
            e = jnp.exp2(jnp.abs(d) * (-LOG2E)).astype(BF16)
        scores.append((lax.dot_general(q * e, k * e, NT_DIMS, preferred_element_type=F32), masks[m]))
        m *= 2
    state_t = yield None

    attn = scores[0][0] * scores[0][1]
    for s, mask in scores[1:]:
        attn = attn + s * mask
    o = (jnp.dot(attn.astype(BF16), v, preferred_element_type=F32)
         + lax.dot_general(q_hat, state_t.astype(BF16), NT_DIMS, preferred_element_type=F32))
    new_state = state_t * jnp.exp(b_edge) + upd
    yield o, new_state


def _hgrn_kernel(q_ref, gf_ref, gb_ref, v_ref, of_ref, ob_ref):
    S = q_ref.shape[0]
    C = HG_CHUNK
    n = S // C
    row = lax.broadcasted_iota(jnp.int32, (C, C), 0)
    col = lax.broadcasted_iota(jnp.int32, (C, C), 1)
    tri_f = jnp.where(col <= row, 1.0, 0.0).astype(BF16)
    tri_b = jnp.where(col >= row, 1.0, 0.0).astype(BF16)
    per_trip = math.gcd(n, 8)

    def scan(block):
        masks_f = _level_masks(C, False, block)
        masks_b = _level_masks(C, True, block)

        def body(t, carry):
            states = list(carry)
            chains = []
            for u in range(per_trip):
                cf = t * per_trip + u
                for rev, g_ref, tri, masks in ((0, gf_ref, tri_f, masks_f), (1, gb_ref, tri_b, masks_b)):
                    rows = pl.ds(pl.multiple_of((n - 1 - cf if rev else cf) * C, C), C)
                    chains.append((rev, rows, _hgrn_chunk(q_ref[rows, :], g_ref[rows, :], v_ref[rows, :],
                                                          tri, masks, bool(rev), block)))
            for _ in range(2):
                for _, _, gen in chains:
                    next(gen)
            for rev, rows, gen in chains:
                o, states[rev] = gen.send(states[rev])
                (ob_ref if rev else of_ref)[rows, :] = o
            return tuple(states)

        z = jnp.zeros((LANES, LANES), F32)
        lax.fori_loop(0, n // per_trip, body, (z, z))

    def min_block_sum(g_ref):
        tot = jnp.sum(g_ref[...].reshape(S // HG_BLOCK, HG_BLOCK, LANES), axis=1)
        return jnp.min(tot)

    mild = jnp.minimum(min_block_sum(gf_ref), min_block_sum(gb_ref)) >= -HG_BLOCK_RANGE
    pl.when(mild)(functools.partial(scan, HG_BLOCK))
    pl.when(jnp.logical_not(mild))(functools.partial(scan, None))


def _hgrn(pa3, pg3, pc3):
    B, S, _ = pa3.shape
    spec = lambda blk: pl.BlockSpec((None, S, LANES), lambda b, h: (b, 0, blk + h))
    out = jax.ShapeDtypeStruct((B, S, HG_HEADS * LANES), F32)
    return pl.pallas_call(
        _hgrn_kernel,
        grid=(B, HG_HEADS),
        in_specs=[spec(12), spec(0), spec(4), spec(0)],
        out_specs=[pl.BlockSpec((None, S, LANES), lambda b, h: (b, 0, h)),
                   pl.BlockSpec((None, S, LANES), lambda b, h: (b, 0, h))],
        out_shape=[out, out],
        compiler_params=_cparams(("parallel", "parallel")),
        name="hgrn2",
    )(pa3, pg3, pg3, pc3)


def _mm_kernel(x_ref, w_ref, o_ref):
    o_ref[...] = jnp.dot(x_ref[...].astype(BF16), w_ref[...],
                         preferred_element_type=F32).astype(o_ref.dtype)


def _matmul(x, w, layer, tm, tn):
    M, K = x.shape
    N = w.shape[2]
    return pl.pallas_call(
        _mm_kernel,
        grid=(M // tm, N // tn),
        in_specs=[pl.BlockSpec((tm, K), lambda i, j: (i, 0)),
                  pl.BlockSpec((None, K, tn), lambda i, j: (layer, 0, j))],
        out_specs=pl.BlockSpec((tm, tn), lambda i, j: (i, j)),
        out_shape=jax.ShapeDtypeStruct((M, N), BF16),
        compiler_params=_cparams(("parallel", "parallel")),
        name="mem_kv_proj",
    )(x, w)


def _outproj_kernel(da_ref, of_ref, ob_ref, gate_ref, hgn_ref, mq_ref, mkv_ref, w_ref, x_ref,
                    g_ref, b_ref, o_ref, obf_ref):
    hg = of_ref[...] + ob_ref[...]
    gate = gate_ref[...].astype(F32)
    parts = [da_ref[...]]
    for h in range(HG_HEADS):
        sl = slice(h * LANES, (h + 1) * LANES)
        y = hg[:, sl]
        y = y * lax.rsqrt(jnp.mean(y * y, axis=-1, keepdims=True) + RMS_EPS) * hgn_ref[...]
        parts.append((y * gate[:, sl]).astype(BF16))
    width = MX_HEADS * LANES
    for h in range(MX_HEADS):
        sl = slice(h * LANES, (h + 1) * LANES)
        s = lax.dot_general(mq_ref[:, sl], mkv_ref[:, sl], NT_DIMS, preferred_element_type=F32)
        p = jnp.exp(s - jnp.max(s, axis=-1, keepdims=True))
        p = p * (1.0 / jnp.sum(p, axis=-1, keepdims=True))
        mv = mkv_ref[:, width + h * LANES:width + (h + 1) * LANES]
        parts.append(jnp.dot(p.astype(BF16), mv, preferred_element_type=F32).astype(BF16))
    mix = jnp.dot(jnp.concatenate(parts, axis=1), w_ref[...], preferred_element_type=F32)
    y = _layer_norm_rows(DEEPNORM_ALPHA * x_ref[...] + mix, g_ref[...], b_ref[...])
    o_ref[...] = y
    obf_ref[...] = y.astype(BF16)


def _outproj(da_o, o_f, o_b, pc, hg_norm_g, mkv3, w_out, layer, x, g, b, tm, seq):
    T, D = x.shape
    W = SEC
    M = mkv3.shape[1]
    per_seq = seq // tm
    row = lambda i: (i, 0)
    const = lambda i: (0, 0)
    return pl.pallas_call(
        _outproj_kernel,
        grid=(T // tm,),
        in_specs=[pl.BlockSpec((tm, W), row), pl.BlockSpec((tm, W), row), pl.BlockSpec((tm, W), row),
                  pl.BlockSpec((tm, W), lambda i: (i, 1)),
                  pl.BlockSpec((1, LANES), const),
                  pl.BlockSpec((tm, W), lambda i: (i, 2)),
                  pl.BlockSpec((None, M, 2 * W), lambda i: (i // per_seq, 0, 0)),
                  pl.BlockSpec((None, 3 * W, D), lambda i: (layer, 0, 0)),
                  pl.BlockSpec((tm, D), row),
                  pl.BlockSpec((1, D), const), pl.BlockSpec((1, D), const)],
        out_specs=[pl.BlockSpec((tm, D), row), pl.BlockSpec((tm, D), row)],
        out_shape=[jax.ShapeDtypeStruct((T, D), F32), jax.ShapeDtypeStruct((T, D), BF16)],
        compiler_params=_cparams(("parallel",)),
        name="outproj_ln",
    )(da_o, o_f, o_b, pc, hg_norm_g.reshape(1, LANES), pc, mkv3, w_out, x, g.reshape(1, D),
      b.reshape(1, D))


FF_TILE = 256
HALO = 16


def _ffn_kernel(xp_ref, xm_ref, xn_ref, xr_ref, wu_ref, cw_ref, cb_ref, wd_ref, g_ref, b_ref,
                o_ref, obf_ref, xcat_ref, act_ref, *, tiles_per_seq):
    i = pl.program_id(0)
    tm = xm_ref.shape[0]
    first = (i % tiles_per_seq) == 0
    last = (i % tiles_per_seq) == tiles_per_seq - 1
    xcat_ref[0:HALO, :] = jnp.where(first, jnp.zeros_like(xp_ref[...]), xp_ref[...])
    xcat_ref[HALO:HALO + tm, :] = xm_ref[...]
    xcat_ref[HALO + tm:, :] = jnp.where(last, jnp.zeros_like(xn_ref[...]), xn_ref[...])
    xc = xcat_ref[...]

    for j in range(D_FF // FF_TILE):
        halves = []
        for part, base in enumerate((0, D_FF)):
            cols = slice(base + j * FF_TILE, base + (j + 1) * FF_TILE)
            u = jnp.dot(xc, wu_ref[:, cols], preferred_element_type=F32)
            rows = u.shape[0]
            cw = cw_ref[:, cols]
            halves.append(pltpu.roll(u, 1, axis=0)[HALO:HALO + tm, :] * cw[0:1, :]
                          + u[HALO:HALO + tm, :] * cw[1:2, :]
                          + pltpu.roll(u, rows - 1, axis=0)[HALO:HALO + tm, :] * cw[2:3, :]
                          + cb_ref[:, cols])
        gate, val = halves
        act_ref[:, j * FF_TILE:(j + 1) * FF_TILE] = (gate * _sigmoid(gate) * val).astype(BF16)

    n_half = 2 if tm % 32 == 0 else 1
    for r in range(n_half):
        rows = slice(r * (tm // n_half), (r + 1) * (tm // n_half))
        ffn = jnp.dot(act_ref[rows, :], wd_ref[...], preferred_element_type=F32)
        y = _layer_norm_rows(DEEPNORM_ALPHA * xr_ref[rows, :] + ffn, g_ref[...], b_ref[...])
        o_ref[rows, :] = y
        obf_ref[rows, :] = y.astype(BF16)


def _ffn(xb, x, w_up, conv_w, conv_b, w_down, layer, g, b, tm, seq):
    T, D = x.shape
    hb = tm // HALO
    nhb = T // HALO
    row = lambda i: (i, 0)
    const = lambda i: (0, 0)
    resident = lambda shape: pl.BlockSpec(shape, const, pipeline_mode=pl.Buffered(1))
    stacked = lambda shape: pl.BlockSpec((None,) + shape, lambda i: (layer, 0, 0),
                                         pipeline_mode=pl.Buffered(1))
    return pl.pallas_call(
        functools.partial(_ffn_kernel, tiles_per_seq=seq // tm),
        grid=(T // tm,),
        in_specs=[pl.BlockSpec((HALO, D), lambda i: (jnp.maximum(i * hb - 1, 0), 0)),
                  pl.BlockSpec((tm, D), row),
                  pl.BlockSpec((HALO, D), lambda i: (jnp.minimum((i + 1) * hb, nhb - 1), 0)),
                  pl.BlockSpec((tm, D), row),
                  stacked((D, 2 * D_FF)),
                  resident((3, 2 * D_FF)),
                  resident((1, 2 * D_FF)),
                  stacked((D_FF, D)),
                  pl.BlockSpec((1, D), const), pl.BlockSpec((1, D), const)],
        out_specs=[pl.BlockSpec((tm, D), row), pl.BlockSpec((tm, D), row)],
        out_shape=[jax.ShapeDtypeStruct((T, D), F32), jax.ShapeDtypeStruct((T, D), BF16)],
        scratch_shapes=[pltpu.VMEM((tm + 2 * HALO, D), BF16),
                        pltpu.VMEM((tm, D_FF), BF16)],
        compiler_params=_cparams(("parallel",)),
        name="conv_ffn_ln",
    )(xb, xb, xb, x, w_up, conv_w, conv_b.reshape(1, -1), w_down, g.reshape(1, D), b.reshape(1, D))


def _rope_tables(positions):
    half = ROPE_DIM // 2
    inv_freq = ROPE_THETA ** (-jnp.arange(0, ROPE_DIM, 2, dtype=F32) / ROPE_DIM)
    ang = positions.astype(F32).reshape(-1, 1) * inv_freq[None, :]
    j = jnp.arange(LANES) % DA_HEAD_DIM
    rotated = (j < ROPE_DIM)[None, :]
    sign = jnp.where(j < half, -1.0, 1.0).astype(F32)[None, :]
    reps = LANES // half
    cosf = jnp.where(rotated, jnp.tile(jnp.cos(ang), (1, reps)), 1.0)
    sinf = jnp.where(rotated, jnp.tile(jnp.sin(ang), (1, reps)) * sign, 0.0)
    return cosf, sinf


def _pick(n, target):
    t = min(n, target)
    while n % t:
        t //= 2
    return t


def kernel(x, mem, positions, ln_in_g, ln_in_b, w_in, da_lambda, da_norm_g, hg_lb_fwd, hg_lb_bwd,
           hg_norm_g, w_mem_kv, w_out, ln1_g, ln1_b, w_up, conv_w, conv_b, w_down, ln2_g, ln2_b):
    B, S, D = x.shape
    M = mem.shape[1]
    T = B * S
    tm = _pick(S, 1024)
    cosf, sinf = _rope_tables(positions)
    mem2 = mem.reshape(B * M, D)
    w_in_b, w_kv_b, w_out_b = w_in.astype(BF16), w_mem_kv.astype(BF16), w_out.astype(BF16)
    w_up_b, w_down_b = w_up.astype(BF16), w_down.astype(BF16)

    for l in range(DEPTH):
        if l == 0:
            pa, pg, pc, h = _inproj(x.reshape(T, D), w_in_b, cosf, sinf, hg_lb_fwd, hg_lb_bwd, l,
                                    _pick(S, 512), ln=(ln_in_g, ln_in_b))
        else:
            pa, pg, pc = _inproj(hb, w_in_b, cosf, sinf, hg_lb_fwd, hg_lb_bwd, l, tm)
        pa3 = pa.reshape(B, S, 4 * SEC)
        da_o = _diff_attn(pa3, da_lambda[l], da_norm_g[l], l, _pick(S, 256), _pick(S, 512))
        o_f, o_b = _hgrn(pa3, pg.reshape(B, S, 2 * SEC), pc.reshape(B, S, 3 * SEC))
        mkv = _matmul(mem2, w_kv_b, l, _pick(B * M, 512), 512)
        h, hb = _outproj(da_o.reshape(T, SEC), o_f.reshape(T, SEC), o_b.reshape(T, SEC), pc,
                         hg_norm_g[l], mkv.reshape(B, M, 2 * SEC), w_out_b, l, h, ln1_g[l],
                         ln1_b[l], _pick(S, 512), S)
        h, hb = _ffn(hb, h, w_up_b, conv_w[l], conv_b[l], w_down_b, l, ln2_g[l], ln2_b[l],
                     _pick(S, 512), S)
    return h.reshape(B, S, D)
```
